```python
import math
import jax
import jax.numpy as jnp
from jax import lax
import numpy as np

D_MODEL = 1024
BATCH = 16
SEQ = 256
DEPTH = 2
DEC_BATCH = 8
DEC_SEQ = 4096
PAST_LEN = 512

GRID_W = 64
N_BRANCH = 4
BRANCH_W = D_MODEL // N_BRANCH
HEAD_DIM = 64
CHUNK = 128
GM_GROUPS = BRANCH_W // HEAD_DIM
NA_HEADS = BRANCH_W // HEAD_DIM
NA_KH = 8
NA_KW = 16
SSM_H = 16
SSM_G = BRANCH_W // SSM_H
SSM_P = 64
GQA_HEADS = BRANCH_W // HEAD_DIM
GQA_KV = 2
GQA_REP = GQA_HEADS // GQA_KV
Q_BLOCK = 128
ROPE_BASE = 10000.0
EPS = 1e-6
NEG_INF = -1e30
PROJ_W = 11 * BRANCH_W + 2 * GQA_KV * HEAD_DIM + N_BRANCH * D_MODEL

kernel_name = 'hybrid_flow_prefix_trunk_step'


def _split_points():
    sizes = (BRANCH_W, BRANCH_W, BRANCH_W,
             BRANCH_W, BRANCH_W, BRANCH_W, BRANCH_W,
             BRANCH_W, BRANCH_W,
             BRANCH_W, GQA_KV * HEAD_DIM, GQA_KV * HEAD_DIM, BRANCH_W,
             N_BRANCH * D_MODEL)
    pts, acc = [], 0
    for s in sizes[:-1]:
        acc += s
        pts.append(acc)
    return pts


def rmsnorm(x, g):
    xf = x.astype(jnp.float32)
    y = xf * lax.rsqrt(jnp.mean(xf * xf, axis=-1, keepdims=True) + EPS)
    return y.astype(x.dtype) * g


def rope_2d(x):
    L = x.shape[1]
    t = jnp.arange(L)
    row = (t // GRID_W).astype(jnp.float32)
    col = (t % GRID_W).astype(jnp.float32)
    half = HEAD_DIM // 2
    nf = half // 2
    freqs = ROPE_BASE ** (-jnp.arange(nf, dtype=jnp.float32) / nf)

    def rot(xh, pos):
        ang = pos[:, None] * freqs[None, :]
        cos = jnp.cos(ang)[None, :, None, :]
        sin = jnp.sin(ang)[None, :, None, :]
        x1, x2 = xh[..., :nf], xh[..., nf:]
        return jnp.concatenate([x1 * cos - x2 * sin, x2 * cos + x1 * sin], axis=-1)

    out = jnp.concatenate([rot(x[..., :half], row), rot(x[..., half:], col)], axis=-1)
    return out.astype(x.dtype)


def block_attention(q, ks, vs):
    B, L, KVH, G, dh = q.shape
    nb = L // Q_BLOCK
    qb = q.reshape(B, nb, Q_BLOCK, KVH, G, dh).transpose(1, 0, 2, 3, 4, 5)
    lens = [k.shape[1] for k in ks]
    scale = HEAD_DIM ** -0.5

    def one(qi):
        s = jnp.concatenate([jnp.einsum('bqkgd,blkd->bkgql', qi, k).astype(jnp.float32) for k in ks], axis=-1)
        p = jax.nn.softmax(s * scale, axis=-1).astype(q.dtype)
        out, off = None, 0
        for k_len, v in zip(lens, vs):
            term = jnp.einsum('bkgql,blkd->bqkgd', p[..., off:off + k_len], v)
            out = term if out is None else out + term
            off += k_len
        return out

    o = lax.map(one, qb)
    return o.transpose(1, 0, 2, 3, 4, 5).reshape(B, L, KVH * G * dh)


def na_latent(q, k, v, k_ctx, v_ctx, rel_bias):
    B, L, H, dh = q.shape
    rows = L // GRID_W
    kh = min(NA_KH, rows)
    kw = NA_KW
    qg = q.reshape(B, rows, GRID_W, H, dh)
    kg = k.reshape(B, rows, GRID_W, H, dh)
    vg = v.reshape(B, rows, GRID_W, H, dh)
    r = jnp.arange(rows)
    start_r = jnp.clip(r - kh // 2, 0, rows - kh)
    row_idx = start_r[:, None] + jnp.arange(kh)[None, :]
    k_rows = kg[:, row_idx]
    v_rows = vg[:, row_idx].reshape(B, rows, kh * GRID_W, H, dh)
    cq = jnp.arange(GRID_W)
    start_c = jnp.clip(cq - kw // 2, 0, GRID_W - kw)
    in_win = (cq[None, :] >= start_c[:, None]) & (cq[None, :] < start_c[:, None] + kw)
    dr_idx = row_idx - r[:, None] + (NA_KH - 1)
    dc_idx = jnp.clip(cq[None, :] - cq[:, None], -(kw - 1), kw - 1) + (kw - 1)
    bias = rel_bias[:, dr_idx[:, None, :, None], dc_idx[None, :, None, :]]
    bias = jnp.transpose(bias, (1, 0, 2, 3, 4)).astype(jnp.float32)
    bias = jnp.where(in_win[None, None, :, None, :], bias, NEG_INF)
    scale = HEAD_DIM ** -0.5
    s_win = jnp.einsum('brqhd,brjkhd->brhqjk', qg, k_rows).astype(jnp.float32) * scale + bias[None]
    s_win = s_win.reshape(B, rows, H, GRID_W, kh * GRID_W)
    s_ctx = jnp.einsum('brqhd,bchd->brhqc', qg, k_ctx).astype(jnp.float32) * scale
    p = jax.nn.softmax(jnp.concatenate([s_win, s_ctx], axis=-1), axis=-1).astype(v.dtype)
    nw = kh * GRID_W
    o = jnp.einsum('brhqk,brkhd->brqhd', p[..., :nw], v_rows) + jnp.einsum('brhqc,bchd->brqhd', p[..., nw:], v_ctx)
    return o.reshape(B, L, H * dh)


def gmlp_mixer(u, v, g_v, ws, b):
    B, L, _ = u.shape
    vn = rmsnorm(v, g_v).reshape(B, L // CHUNK, CHUNK, GM_GROUPS, HEAD_DIM)
    sp = jnp.einsum('gqp,bnpgd->bnqgd', ws, vn) + b.T[None, None, :, :, None]
    return u * sp.reshape(B, L, BRANCH_W)


def _cmul(ar, ai, br, bi):
    return ar * br - ai * bi, ar * bi + ai * br


def _scan_op(e1, e2):
    a1r, a1i, b1r, b1i = e1
    a2r, a2i, b2r, b2i = e2
    ar, ai = _cmul(a2r, a2i, a1r, a1i)
    br, bi = _cmul(a2r, a2i, b1r, b1i)
    return ar, ai, br + b2r, bi + b2i


def s5_mixer(x, lp, s0):
    B, L, _ = x.shape
    u = x.astype(jnp.float32).reshape(B, L, SSM_G, SSM_H)
    ys, finals = [], []
    for d in range(2):
        reverse = d == 1
        dt = jnp.exp(lp['ssm_log_dt'][d].astype(jnp.float32))[:, None]
        a_re = lp['ssm_a_re'][d].astype(jnp.float32)
        a_im = lp['ssm_a_im'][d].astype(jnp.float32)
        mag = jnp.exp(dt * a_re)
        ab_re, ab_im = mag * jnp.cos(dt * a_im), mag * jnp.sin(dt * a_im)
        den = a_re * a_re + a_im * a_im
        nr, ni = ab_re - 1.0, ab_im
        coef_re = (nr * a_re + ni * a_im) / den
        coef_im = (ni * a_re - nr * a_im) / den
        bb_re, bb_im = _cmul(coef_re[..., None], coef_im[..., None],
                             lp['ssm_b_re'][d].astype(jnp.float32), lp['ssm_b_im'][d].astype(jnp.float32))
        bu_re = jnp.einsum('blgi,gpi->blgp', u, bb_re)
        bu_im = jnp.einsum('blgi,gpi->blgp', u, bb_im)
        shp = bu_re.shape
        _, _, s_re, s_im = lax.associative_scan(
            _scan_op, (jnp.broadcast_to(ab_re, shp), jnp.broadcast_to(ab_im, shp), bu_re, bu_im),
            reverse=reverse, axis=1)
        if s0 is None:
            idx = 0 if reverse else L - 1
            finals.append(jnp.stack([s_re[:, idx], s_im[:, idx]], axis=-1))
        else:
            t = jnp.arange(L, dtype=jnp.float32)
            n = (L - t) if reverse else (t + 1.0)
            pm = jnp.exp(n[:, None, None] * (dt * a_re)[None])
            ph = n[:, None, None] * (dt * a_im)[None]
            init = s0[:, d].astype(jnp.float32)
            add_re, add_im = _cmul((pm * jnp.cos(ph))[None], (pm * jnp.sin(ph))[None],
                                   init[:, None, ..., 0], init[:, None, ..., 1])
            s_re = s_re + add_re
            s_im = s_im + add_im
        ys.append(jnp.einsum('blgp,gip->blgi', s_re, lp['ssm_c_re'][d].astype(jnp.float32))
                  - jnp.einsum('blgp,gip->blgi', s_im, lp['ssm_c_im'][d].astype(jnp.float32)))
    y = (ys[0] + ys[1]).reshape(B, L, BRANCH_W) + lp['ssm_d'] * x.astype(jnp.float32)
    y = jax.nn.gelu(y)
    y = y * jax.nn.sigmoid(y @ lp['w_glu'] + lp['b_glu'])
    new_state = jnp.stack(finals, axis=1) if s0 is None else None
    return y.astype(x.dtype), new_state


def trunk_layer(x, cond, lp, cache):
    B, L, _ = x.shape
    mod = jax.nn.silu(cond) @ lp['w_ada'] + lp['b_ada']
    if mod.ndim == 2:
        mod = mod[:, None, :]
    shift, scale, gate = jnp.split(mod, 3, axis=-1)
    h = rmsnorm(x, lp['norm_g']) * (1.0 + scale) + shift
    (a_u, a_v, a_z, b_q, b_k, b_v, b_z, c_x, c_z, d_q, d_k, d_v, d_z, g_merge) = \
        jnp.split(h @ lp['w_in'], _split_points(), axis=-1)
    y_a = gmlp_mixer(a_u, a_v, lp['gm_v_g'], lp['gm_ws'], lp['gm_b']) * jax.nn.silu(a_z)
    q_na = b_q.reshape(B, L, NA_HEADS, HEAD_DIM)
    k_na = b_k.reshape(B, L, NA_HEADS, HEAD_DIM)
    v_na = b_v.reshape(B, L, NA_HEADS, HEAD_DIM)
    if cache is None:
        y_b = block_attention(q_na[:, :, :, None, :], [k_na], [v_na])
    else:
        y_b = na_latent(q_na, k_na, v_na, cache['na_k'], cache['na_v'], lp['na_rel_bias'])
    y_b = y_b * jax.nn.silu(b_z)
    y_c, s_fin = s5_mixer(c_x, lp, None if cache is None else cache['ssm'])
    y_c = y_c * jax.nn.silu(c_z)
    q_d = rmsnorm(d_q.reshape(B, L, GQA_HEADS, HEAD_DIM), lp['gqa_q_g'])
    k_d = rmsnorm(d_k.reshape(B, L, GQA_KV, HEAD_DIM), lp['gqa_k_g'])
    v_d = d_v.reshape(B, L, GQA_KV, HEAD_DIM)
    if cache is None:
        y_d = block_attention(q_d.reshape(B, L, GQA_KV, GQA_REP, HEAD_DIM), [k_d], [v_d])
    else:
        y_d = block_attention(rope_2d(q_d).reshape(B, L, GQA_KV, GQA_REP, HEAD_DIM),
                              [rope_2d(k_d), cache['gqa_k']], [v_d, cache['gqa_v']])
    y_d = y_d * jax.nn.silu(d_z)
    merged = None
    for n, y_n in enumerate((y_a, y_b, y_c, y_d)):
        term = jax.nn.sigmoid(g_merge[..., n * D_MODEL:(n + 1) * D_MODEL]) * (y_n @ lp['w_branch'][n])
        merged = term if merged is None else merged + term
    x = x + gate * (merged @ lp['w_out'])
    if cache is None:
        return x, (k_na, v_na, k_d, v_d, s_fin)
    return x, None


def setup_inputs(seed: int = 0) -> dict:
    key = jax.random.key(seed)
    ks = jax.random.split(key, 32)
    f32 = jnp.float32

    def nrm(k, shape, s):
        return jax.random.normal(k, shape, f32) * s

    a_im = math.pi * jnp.arange(SSM_P, dtype=f32)
    return {
        'x_prompt': nrm(ks[0], (BATCH, SEQ, D_MODEL), 1.0),
        'x_sample': nrm(ks[1], (DEC_BATCH, DEC_SEQ, D_MODEL), 1.0),
        'c': nrm(ks[2], (DEC_BATCH, D_MODEL), 1.0),
        'cache_na_k': nrm(ks[3], (DEC_BATCH, DEPTH, PAST_LEN, NA_HEADS, HEAD_DIM), 1.0),
        'cache_na_v': nrm(ks[4], (DEC_BATCH, DEPTH, PAST_LEN, NA_HEADS, HEAD_DIM), 1.0),
        'cache_gqa_k': nrm(ks[5], (DEC_BATCH, DEPTH, PAST_LEN, GQA_KV, HEAD_DIM), 1.0),
        'cache_gqa_v': nrm(ks[6], (DEC_BATCH, DEPTH, PAST_LEN, GQA_KV, HEAD_DIM), 1.0),
        'state_ssm': nrm(ks[7], (DEC_BATCH, DEPTH, 2, SSM_G, SSM_P, 2), 0.1),
        'c_ctx': nrm(ks[8], (D_MODEL,), 1.0),
        'norm_g': 1.0 + nrm(ks[9], (DEPTH, D_MODEL), 0.02),
        'w_ada': nrm(ks[10], (DEPTH, D_MODEL, 3 * D_MODEL), 0.5 * D_MODEL ** -0.5),
        'b_ada': nrm(ks[11], (DEPTH, 3 * D_MODEL), 0.01),
        'w_in': nrm(ks[12], (DEPTH, D_MODEL, PROJ_W), D_MODEL ** -0.5),
        'gm_v_g': 1.0 + nrm(ks[13], (DEPTH, BRANCH_W), 0.02),
        'gm_ws': nrm(ks[14], (DEPTH, GM_GROUPS, CHUNK, CHUNK), CHUNK ** -0.5),
        'gm_b': 1.0 + nrm(ks[15], (DEPTH, GM_GROUPS, CHUNK), 0.02),
        'na_rel_bias': nrm(ks[16], (DEPTH, NA_HEADS, 2 * NA_KH - 1, 2 * NA_KW - 1), 0.1),
        'ssm_a_re': -0.5 + nrm(ks[17], (DEPTH, 2, SSM_G, SSM_P), 0.01),
        'ssm_a_im': a_im + nrm(ks[18], (DEPTH, 2, SSM_G, SSM_P), 0.01),
        'ssm_log_dt': jax.random.uniform(ks[19], (DEPTH, 2, SSM_G), f32, math.log(0.001), math.log(0.1)),
        'ssm_b_re': nrm(ks[20], (DEPTH, 2, SSM_G, SSM_P, SSM_H), (2 * SSM_H) ** -0.5),
        'ssm_b_im': nrm(ks[21], (DEPTH, 2, SSM_G, SSM_P, SSM_H), (2 * SSM_H) ** -0.5),
        'ssm_c_re': nrm(ks[22], (DEPTH, 2, SSM_G, SSM_H, SSM_P), (2 * SSM_P) ** -0.5),
        'ssm_c_im': nrm(ks[23], (DEPTH, 2, SSM_G, SSM_H, SSM_P), (2 * SSM_P) ** -0.5),
        'ssm_d': nrm(ks[24], (DEPTH, BRANCH_W), 0.5),
        'w_glu': nrm(ks[25], (DEPTH, BRANCH_W, BRANCH_W), BRANCH_W ** -0.5),
        'b_glu': nrm(ks[26], (DEPTH, BRANCH_W), 0.01),
        'gqa_q_g': 1.0 + nrm(ks[27], (DEPTH, HEAD_DIM), 0.02),
        'gqa_k_g': 1.0 + nrm(ks[28], (DEPTH, HEAD_DIM), 0.02),
        'w_branch': nrm(ks[29], (DEPTH, N_BRANCH, BRANCH_W, D_MODEL), BRANCH_W ** -0.5),
        'w_out': nrm(ks[30], (DEPTH, D_MODEL, D_MODEL), D_MODEL ** -0.5),
        'final_g': 1.0 + nrm(ks[31], (D_MODEL,), 0.02),
    }


def reference(x_prompt, x_sample, c, cache_na_k, cache_na_v, cache_gqa_k, cache_gqa_v, state_ssm, c_ctx,
              norm_g, w_ada, b_ada, w_in, gm_v_g, gm_ws, gm_b, na_rel_bias, ssm_a_re, ssm_a_im, ssm_log_dt,
              ssm_b_re, ssm_b_im, ssm_c_re, ssm_c_im, ssm_d, w_glu, b_glu, gqa_q_g, gqa_k_g, w_branch, w_out,
              final_g):
    xp, xs = x_prompt, x_sample
    na_k_l, na_v_l, gqa_k_l, gqa_v_l, ssm_l = [], [], [], [], []
    for l in range(DEPTH):
        lp = dict(norm_g=norm_g[l], w_ada=w_ada[l], b_ada=b_ada[l], w_in=w_in[l], gm_v_g=gm_v_g[l],
                  gm_ws=gm_ws[l], gm_b=gm_b[l], na_rel_bias=na_rel_bias[l], ssm_a_re=ssm_a_re[l],
                  ssm_a_im=ssm_a_im[l], ssm_log_dt=ssm_log_dt[l], ssm_b_re=ssm_b_re[l], ssm_b_im=ssm_b_im[l],
                  ssm_c_re=ssm_c_re[l], ssm_c_im=ssm_c_im[l], ssm_d=ssm_d[l], w_glu=w_glu[l], b_glu=b_glu[l],
                  gqa_q_g=gqa_q_g[l], gqa_k_g=gqa_k_g[l], w_branch=w_branch[l], w_out=w_out[l])
        xp, (k_na, v_na, k_g, v_g, s_f) = trunk_layer(xp, c_ctx, lp, None)
        na_k_l.append(k_na)
        na_v_l.append(v_na)
        gqa_k_l.append(k_g)
        gqa_v_l.append(v_g)
        ssm_l.append(s_f)
        cache = dict(na_k=cache_na_k[:, l], na_v=cache_na_v[:, l], gqa_k=cache_gqa_k[:, l],
                     gqa_v=cache_gqa_v[:, l], ssm=state_ssm[:, l])
        xs, _ = trunk_layer(xs, c, lp, cache)
    y_prompt = rmsnorm(xp, final_g)
    y_sample = rmsnorm(xs, final_g)
    return (y_prompt, y_sample, jnp.stack(na_k_l, axis=1), jnp.stack(na_v_l, axis=1),
            jnp.stack(gqa_k_l, axis=1), jnp.stack(gqa_v_l, axis=1), jnp.stack(ssm_l, axis=1))
```

```python
import functools
import math

import numpy as np
import jax
import jax.numpy as jnp
from jax import lax
from jax.experimental import pallas as pl
from jax.experimental.pallas import tpu as pltpu

D_MODEL = 1024
DEPTH = 2
GRID_W = 64
BRANCH_W = 256
HEAD_DIM = 64
CHUNK = 128
GM_GROUPS = 4
NA_HEADS = 4
NA_KH = 8
NA_KW = 16
SSM_H = 16
SSM_G = 16
SSM_P = 64
SSM_N = SSM_G * SSM_P
GQA_HEADS = 4
GQA_KV = 2
ROPE_BASE = 10000.0
EPS = 1e-6
NEG_INF = -1e30
ATTN_SCALE = HEAD_DIM ** -0.5

SUBLANES = 8
VMEM_LIMIT = 56 * 1024 * 1024

F32 = jnp.float32
BF16 = jnp.bfloat16


def _params(*sem):
    return pltpu.CompilerParams(dimension_semantics=sem, vmem_limit_bytes=VMEM_LIMIT)


def _dot(a, b):
    return jnp.dot(a, b, preferred_element_type=F32)


def _dot_nt(a, b):
    return lax.dot_general(a, b, (((1,), (1,)), ((), ())), preferred_element_type=F32)


def _full(shape):
    nd = len(shape)
    return pl.BlockSpec(shape, lambda *_: (0,) * nd)


def _modulated_norm(x, g, scale, shift):
    y = x * lax.rsqrt(jnp.mean(x * x, axis=-1, keepdims=True) + EPS)
    return (y * g) * (1.0 + scale) + shift


def _group_sumsq(x, ones_blk):
    x2 = x * x
    hi = x2.astype(BF16)
    lo = (x2 - hi.astype(F32)).astype(BF16)
    return _dot(hi, ones_blk) + _dot(lo, ones_blk)


def _head_rmsnorm(x, ones_blk, g):
    ss = _group_sumsq(x, ones_blk)
    return (x * lax.rsqrt(ss * (1.0 / HEAD_DIM) + EPS)) * g


def _rope(x, cos, sin_signed):
    w = x.shape[1]
    lane = lax.broadcasted_iota(jnp.int32, x.shape, 1)
    up = pltpu.roll(x, 16, axis=1)
    dn = pltpu.roll(x, w - 16, axis=1)
    partner = jnp.where((lane & 16) != 0, up, dn)
    return x * cos + partner * sin_signed


def _mod_kernel(cond_ref, w_ref, b_ref, o_ref):
    c = cond_ref[...]
    s = (c * jax.nn.sigmoid(c)).astype(BF16)
    o_ref[...] = _dot(s, w_ref[...].astype(BF16)) + b_ref[...]


def _modulation(cond, w_ada, b_ada):
    r = cond.shape[0]
    nj = 3
    return pl.pallas_call(
        _mod_kernel,
        grid=(DEPTH, nj),
        in_specs=[
            pl.BlockSpec((r, D_MODEL), lambda l, j: (0, 0)),
            pl.BlockSpec((None, D_MODEL, D_MODEL), lambda l, j: (l, 0, j)),
            pl.BlockSpec((None, 1, D_MODEL), lambda l, j: (l, 0, j)),
        ],
        out_specs=pl.BlockSpec((None, r, D_MODEL), lambda l, j: (l, 0, j)),
        out_shape=jax.ShapeDtypeStruct((DEPTH, r, 3 * D_MODEL), F32),
        compiler_params=_params("arbitrary", "arbitrary"),
        name="adaln_modulation",
    )(cond, w_ada, b_ada.reshape(DEPTH, 1, 3 * D_MODEL))


def _inproj_kernel(x_ref, g_ref, scale_ref, shift_ref, w_ref, gmg_ref, gmw_ref, gmb_ref, ones_ref,
                   qg_ref, kg_ref, *rest, tm, latent):
    if latent:
        cos_ref, sin_ref = rest[:2]
        rest = rest[2:]
    ya_ref, bq_ref, bk_ref, bv_ref, cx_ref, dq_ref, dk_ref, dv_ref = rest

    h = _modulated_norm(x_ref[...], g_ref[...], scale_ref[...], shift_ref[...])
    p = _dot(h.astype(BF16), w_ref[...])

    u = p[:, 0:256]
    v = p[:, 256:512]
    vn = (v * lax.rsqrt(jnp.mean(v * v, axis=-1, keepdims=True) + EPS)) * gmg_ref[...]
    vnb = vn.astype(BF16)
    lane_grp = lax.broadcasted_iota(jnp.int32, (CHUNK, BRANCH_W), 1) // HEAD_DIM
    for c in range(tm // CHUNK):
        r = _dot(gmw_ref[...], vnb[c * CHUNK:(c + 1) * CHUNK, :])
        sp = r[0:CHUNK]
        for g in range(1, GM_GROUPS):
            sp = jnp.where(lane_grp == g, r[g * CHUNK:(g + 1) * CHUNK], sp)
        sp = sp + gmb_ref[...]
        ya_ref[c * CHUNK:(c + 1) * CHUNK, :] = u[c * CHUNK:(c + 1) * CHUNK, :] * sp

    bq_ref[...] = p[:, 512:768].astype(bq_ref.dtype)
    bk_ref[...] = p[:, 768:1024].astype(bk_ref.dtype)
    bv_ref[...] = p[:, 1024:1280].astype(bv_ref.dtype)
    cx_ref[...] = p[:, 1280:1536]
    ones_blk = ones_ref[...]
    dq = _head_rmsnorm(p[:, 1536:1792], ones_blk, qg_ref[...])
    dk = _head_rmsnorm(p[:, 1792:1920], ones_blk[0:128, 0:128], kg_ref[...])
    if latent:
        cos = cos_ref[...]
        sin = sin_ref[...]
        dq = _rope(dq, cos, sin)
        dk = _rope(dk, cos[:, 0:128], sin[:, 0:128])
    dq_ref[...] = dq.astype(dq_ref.dtype)
    dk_ref[...] = dk.astype(dk_ref.dtype)
    dv_ref[...] = p[:, 1920:2048].astype(dv_ref.dtype)


def _inproj(x, mod_scale, mod_shift, lw, consts, *, batch, seq, tm, latent):
    n = batch * seq
    tiles_per_seq = seq // tm
    halves = batch // SUBLANES
    per_batch_mod = mod_scale.shape[0] > 1

    def mod_idx(i):
        return ((i // tiles_per_seq) if per_batch_mod else 0, 0, 0)

    row = lambda i: (i, 0)
    kv_dtype = BF16 if latent else F32
    in_specs = [
        pl.BlockSpec((tm, D_MODEL), row),
        _full((1, D_MODEL)),
        pl.BlockSpec((None, 1, D_MODEL), mod_idx),
        pl.BlockSpec((None, 1, D_MODEL), mod_idx),
        _full((D_MODEL, 2048)),
        _full((1, BRANCH_W)),
        _full((GM_GROUPS * CHUNK, CHUNK)),
        _full((CHUNK, BRANCH_W)),
        _full((BRANCH_W, BRANCH_W)),
        _full((1, BRANCH_W)),
        _full((1, 128)),
    ]
    args = [x, lw["norm_g"], mod_scale, mod_shift, lw["w_proj"], lw["gm_v_g"], lw["gm_w"], lw["gm_bias"],
            consts["ones_blk"], lw["gqa_q_g"], lw["gqa_k_g"]]
    if latent:
        rope_idx = lambda i: (i % tiles_per_seq, 0)
        in_specs += [pl.BlockSpec((tm, BRANCH_W), rope_idx), pl.BlockSpec((tm, BRANCH_W), rope_idx)]
        args += [consts["rope_cos"], consts["rope_sin"]]

    def cx_idx(i):
        b = i // tiles_per_seq
        return (b // SUBLANES, i % tiles_per_seq, b % SUBLANES)

    out_specs = [
        pl.BlockSpec((tm, BRANCH_W), row),
        pl.BlockSpec((tm, BRANCH_W), row),
        pl.BlockSpec((tm, BRANCH_W), row),
        pl.BlockSpec((tm, BRANCH_W), row),
        pl.BlockSpec((None, tm, BRANCH_W), cx_idx),
        pl.BlockSpec((tm, BRANCH_W), row),
        pl.BlockSpec((tm, 128), row),
        pl.BlockSpec((tm, 128), row),
    ]
    out_shape = [
        jax.ShapeDtypeStruct((n, BRANCH_W), F32),
        jax.ShapeDtypeStruct((n, BRANCH_W), BF16),
        jax.ShapeDtypeStruct((n, BRANCH_W), kv_dtype),
        jax.ShapeDtypeStruct((n, BRANCH_W), kv_dtype),
        jax.ShapeDtypeStruct((halves, seq, SUBLANES * BRANCH_W), F32),
        jax.ShapeDtypeStruct((n, BRANCH_W), BF16),
        jax.ShapeDtypeStruct((n, 128), kv_dtype),
        jax.ShapeDtypeStruct((n, 128), kv_dtype),
    ]
    return pl.pallas_call(
        functools.partial(_inproj_kernel, tm=tm, latent=latent),
        grid=(n // tm,),
        in_specs=in_specs,
        out_specs=out_specs,
        out_shape=out_shape,
        compiler_params=_params("parallel"),
        name="inproj_latent" if latent else "inproj_ctx",
    )(*args)


def _softmax_chunk(s, m, l, acc, v):
    m_new = jnp.maximum(m, jnp.max(s, axis=-1, keepdims=True))
    alpha = jnp.exp(m - m_new)
    p = jnp.exp(s - m_new)
    l = alpha * l + jnp.sum(p, axis=-1, keepdims=True)
    acc = alpha * acc + _dot(p.astype(BF16), v)
    return m_new, l, acc


def _block_attn_kernel(q_ref, k_ref, v_ref, *rest, tq, kvh, rep, lk, ck, cache_len):
    if cache_len:
        ck_ref, cv_ref, o_ref = rest
    else:
        (o_ref,) = rest
    rows = rep * tq
    for j in range(kvh):
        qs = jnp.concatenate(
            [q_ref[:, (j * rep + g) * HEAD_DIM:(j * rep + g + 1) * HEAD_DIM] for g in range(rep)], axis=0)
        qs = qs * ATTN_SCALE
        lanes = slice(j * HEAD_DIM, (j + 1) * HEAD_DIM)

        def body(c, carry, lanes=lanes, qs=qs):
            m, l, acc = carry
            r0 = c * ck if isinstance(c, int) else pl.multiple_of(c * ck, ck)
            kc = k_ref[pl.ds(r0, ck), lanes].astype(BF16)
            vc = v_ref[pl.ds(r0, ck), lanes].astype(BF16)
            return _softmax_chunk(_dot_nt(qs, kc), m, l, acc, vc)

        carry = (jnp.full((rows, 1), NEG_INF, F32), jnp.zeros((rows, 1), F32), jnp.zeros((rows, HEAD_DIM), F32))
        if lk // ck == 1:
            carry = body(0, carry)
        else:
            carry = lax.fori_loop(0, lk // ck, body, carry)
        if cache_len:
            m, l, acc = carry
            carry = _softmax_chunk(_dot_nt(qs, ck_ref[:, lanes]), m, l, acc, cv_ref[:, lanes])
        m, l, acc = carry
        o = acc / l
        for g in range(rep):
            h = j * rep + g
            o_ref[:, h * HEAD_DIM:(h + 1) * HEAD_DIM] = o[g * tq:(g + 1) * tq, :]


def _block_attention(q, k, v, cache_k, cache_v, *, batch, seq, kvh, rep, tq, ck, name):
    n = batch * seq
    wkv = kvh * HEAD_DIM
    nq = seq // tq
    cache_len = 0 if cache_k is None else cache_k.shape[1]
    in_specs = [
        pl.BlockSpec((tq, BRANCH_W), lambda b, i: (b * nq + i, 0)),
        pl.BlockSpec((None, seq, wkv), lambda b, i: (b, 0, 0)),
        pl.BlockSpec((None, seq, wkv), lambda b, i: (b, 0, 0)),
    ]
    args = [q, k.reshape(batch, seq, wkv), v.reshape(batch, seq, wkv)]
    if cache_len:
        in_specs += [pl.BlockSpec((None, cache_len, wkv), lambda b, i: (b, 0, 0))] * 2
        args += [cache_k, cache_v]
    return pl.pallas_call(
        functools.partial(_block_attn_kernel, tq=tq, kvh=kvh, rep=rep, lk=seq, ck=ck, cache_len=cache_len),
        grid=(batch, nq),
        in_specs=in_specs,
        out_specs=pl.BlockSpec((tq, BRANCH_W), lambda b, i: (b * nq + i, 0)),
        out_shape=jax.ShapeDtypeStruct((n, BRANCH_W), F32),
        compiler_params=_params("parallel", "arbitrary"),
        name=name,
    )(*args)


NA_QROWS = 4
NA_KROWS = NA_QROWS + NA_KH - 1
NA_TQ = NA_QROWS * GRID_W
NA_TK = NA_KROWS * GRID_W


def _na_kernel(q_ref, k_ref, v_ref, ck_ref, cv_ref, bias_ref, o_ref, *, rows):
    i = pl.program_id(1)
    kbase = jnp.clip(i * NA_QROWS - NA_KH // 2, 0, rows - NA_KROWS)
    r0 = pl.multiple_of(kbase * GRID_W, GRID_W)
    for h in range(NA_HEADS):
        lanes = slice(h * HEAD_DIM, (h + 1) * HEAD_DIM)
        qh = q_ref[:, lanes] * ATTN_SCALE
        s_win = _dot_nt(qh, k_ref[pl.ds(r0, NA_TK), lanes]) + bias_ref[h]
        s_ctx = _dot_nt(qh, ck_ref[:, lanes])
        m = jnp.maximum(jnp.max(s_win, axis=-1, keepdims=True), jnp.max(s_ctx, axis=-1, keepdims=True))
        p_win = jnp.exp(s_win - m)
        p_ctx = jnp.exp(s_ctx - m)
        l = jnp.sum(p_win, axis=-1, keepdims=True) + jnp.sum(p_ctx, axis=-1, keepdims=True)
        acc = _dot(p_win.astype(BF16), v_ref[pl.ds(r0, NA_TK), lanes]) + _dot(p_ctx.astype(BF16), cv_ref[:, lanes])
        o_ref[:, lanes] = acc / l


def _na_attention(q, k, v, cache_k, cache_v, bias, *, batch, seq):
    n = batch * seq
    rows = seq // GRID_W
    nq = rows // NA_QROWS
    cache_len = cache_k.shape[1]

    def bias_idx(b, i):
        return (jnp.where(i == 0, 0, jnp.where(i == nq - 1, 2, 1)), 0, 0, 0)

    return pl.pallas_call(
        functools.partial(_na_kernel, rows=rows),
        grid=(batch, nq),
        in_specs=[
            pl.BlockSpec((NA_TQ, BRANCH_W), lambda b, i: (b * nq + i, 0)),
            pl.BlockSpec((None, seq, BRANCH_W), lambda b, i: (b, 0, 0)),
            pl.BlockSpec((None, seq, BRANCH_W), lambda b, i: (b, 0, 0)),
            pl.BlockSpec((None, cache_len, BRANCH_W), lambda b, i: (b, 0, 0)),
            pl.BlockSpec((None, cache_len, BRANCH_W), lambda b, i: (b, 0, 0)),
            pl.BlockSpec((None, NA_HEADS, NA_TQ, NA_TK), bias_idx),
        ],
        out_specs=pl.BlockSpec((NA_TQ, BRANCH_W), lambda b, i: (b * nq + i, 0)),
        out_shape=jax.ShapeDtypeStruct((n, BRANCH_W), F32),
        compiler_params=_params("parallel", "arbitrary"),
        name="na_latent",
    )(q, k.reshape(batch, seq, BRANCH_W), v.reshape(batch, seq, BRANCH_W), cache_k, cache_v, bias)


def _na_bias_tables(rel_bias, rows):
    nq = rows // NA_QROWS
    kh = min(NA_KH, rows)
    tabs = []
    for blk in (0, 1, nq - 1):
        kbase = int(np.clip(blk * NA_QROWS - NA_KH // 2, 0, rows - NA_KROWS))
        r = blk * NA_QROWS + np.arange(NA_QROWS)
        start_r = np.clip(r - kh // 2, 0, rows - kh)
        kr = kbase + np.arange(NA_KROWS)
        row_ok = (kr[None, :] >= start_r[:, None]) & (kr[None, :] < start_r[:, None] + kh)
        dr = np.clip(kr[None, :] - r[:, None] + (NA_KH - 1), 0, 2 * NA_KH - 2)
        cq = np.arange(GRID_W)
        start_c = np.clip(cq - NA_KW // 2, 0, GRID_W - NA_KW)
        col_ok = (cq[None, :] >= start_c[:, None]) & (cq[None, :] < start_c[:, None] + NA_KW)
        dc = np.clip(cq[None, :] - cq[:, None], -(NA_KW - 1), NA_KW - 1) + (NA_KW - 1)
        ok = row_ok[:, None, :, None] & col_ok[None, :, None, :]
        dr_f = np.broadcast_to(dr[:, None, :, None], ok.shape).reshape(NA_TQ, NA_TK)
        dc_f = np.broadcast_to(dc[None, :, None, :], ok.shape).reshape(NA_TQ, NA_TK)
        t = rel_bias[:, dr_f, dc_f].astype(F32)
        tabs.append(jnp.where(ok.reshape(NA_TQ, NA_TK)[None], t, NEG_INF))
    return jnp.stack(tabs, axis=0)


def _gelu_tanh(y):
    return 0.5 * y * (1.0 + jnp.tanh(math.sqrt(2.0 / math.pi) * (y + 0.044715 * (y * y * y))))


def _s5_kernel(x_ref, bmat_ref, cmat_ref, abar_ref, s0_ref, *rest, steps, reverse):
    if reverse:
        yprev_ref, dvec_ref, wglu_ref, bglu_ref, y_ref, sfin_ref, bu_ref, st_ref = rest
    else:
        y_ref, sfin_ref, bu_ref, st_ref = rest
    j = pl.program_id(1)

    @pl.when(j == 0)
    def _():
        st_ref[...] = s0_ref[...]

    x = x_ref[...]
    bu_ref[...] = _dot(x.astype(BF16), bmat_ref[...])
    a_re = jnp.broadcast_to(abar_ref[:, 0:SSM_N], (SUBLANES, SSM_N))
    a_im = jnp.broadcast_to(abar_ref[:, SSM_N:], (SUBLANES, SSM_N))

    def step(i, carry):
        s_re, s_im = carry
        t = (steps - 1 - i) if reverse else i
        r0 = pl.multiple_of(t * SUBLANES, SUBLANES)
        n_re = a_re * s_re - a_im * s_im + bu_ref[pl.ds(r0, SUBLANES), 0:SSM_N]
        n_im = a_re * s_im + a_im * s_re + bu_ref[pl.ds(r0, SUBLANES), SSM_N:]
        bu_ref[pl.ds(r0, SUBLANES), 0:SSM_N] = n_re
        bu_ref[pl.ds(r0, SUBLANES), SSM_N:] = n_im
        return n_re, n_im

    s_re, s_im = lax.fori_loop(0, steps, step, (st_ref[:, 0:SSM_N], st_ref[:, SSM_N:]))
    st_ref[:, 0:SSM_N] = s_re
    st_ref[:, SSM_N:] = s_im
    sfin_ref[...] = st_ref[...]

    y = _dot(bu_ref[...].astype(BF16), cmat_ref[...])
    if reverse:
        y = yprev_ref[...] + y + dvec_ref[...] * x
        y = _gelu_tanh(y)
        y = y * jax.nn.sigmoid(_dot(y.astype(BF16), wglu_ref[...]) + bglu_ref[...])
    y_ref[...] = y


def _s5_pass(x3, sp, s0, yprev, lw, *, steps, reverse):
    halves, rows, _ = x3.shape
    nchunk = rows // (steps * SUBLANES)
    tr = steps * SUBLANES

    def blk(h, j):
        return (h, (nchunk - 1 - j) if reverse else j, 0)

    in_specs = [
        pl.BlockSpec((None, tr, BRANCH_W), blk),
        _full((BRANCH_W, 2 * SSM_N)),
        _full((2 * SSM_N, BRANCH_W)),
        _full((1, 2 * SSM_N)),
        pl.BlockSpec((None, SUBLANES, 2 * SSM_N), lambda h, j: (h, 0, 0)),
    ]
    args = [x3, sp["bmat"], sp["cmat"], sp["abar"], s0]
    if reverse:
        in_specs += [pl.BlockSpec((None, tr, BRANCH_W), blk), _full((1, BRANCH_W)),
                     _full((BRANCH_W, BRANCH_W)), _full((1, BRANCH_W))]
        args += [yprev, lw["ssm_d"], lw["w_glu"], lw["b_glu"]]
    return pl.pallas_call(
        functools.partial(_s5_kernel, steps=steps, reverse=reverse),
        grid=(halves, nchunk),
        in_specs=in_specs,
        out_specs=[pl.BlockSpec((None, tr, BRANCH_W), blk),
                   pl.BlockSpec((None, SUBLANES, 2 * SSM_N), lambda h, j: (h, 0, 0))],
        out_shape=[jax.ShapeDtypeStruct(x3.shape, F32),
                   jax.ShapeDtypeStruct((halves, SUBLANES, 2 * SSM_N), F32)],
        scratch_shapes=[pltpu.VMEM((tr, 2 * SSM_N), F32), pltpu.VMEM((SUBLANES, 2 * SSM_N), F32)],
        compiler_params=_params("parallel", "arbitrary"),
        name="s5_reverse" if reverse else "s5_forward",
    )(*args)


def _s5_discretise(a_re, a_im, log_dt, b_re, b_im, c_re, c_im):
    dt = jnp.exp(log_dt)[:, None]
    mag = jnp.exp(dt * a_re)
    ab_re, ab_im = mag * jnp.cos(dt * a_im), mag * jnp.sin(dt * a_im)
    den = a_re * a_re + a_im * a_im
    nr, ni = ab_re - 1.0, ab_im
    coef_re = ((nr * a_re + ni * a_im) / den)[..., None]
    coef_im = ((ni * a_re - nr * a_im) / den)[..., None]
    bb_re = coef_re * b_re - coef_im * b_im
    bb_im = coef_re * b_im + coef_im * b_re
    eye = jnp.eye(SSM_G, dtype=F32)

    def in_blocks(bb):
        return jnp.einsum("gpi,gk->gikp", bb, eye).reshape(SSM_G * SSM_H, SSM_N)

    def out_blocks(cc):
        return jnp.einsum("gip,gk->gpki", cc, eye).reshape(SSM_N, SSM_G * SSM_H)

    bmat = jnp.concatenate([in_blocks(bb_re), in_blocks(bb_im)], axis=1).astype(BF16)
    cmat = jnp.concatenate([out_blocks(c_re), -out_blocks(c_im)], axis=0).astype(BF16)
    abar = jnp.concatenate([ab_re.reshape(1, SSM_N), ab_im.reshape(1, SSM_N)], axis=1)
    return dict(bmat=bmat, cmat=cmat, abar=abar)


def _state_to_rows(s):
    b = s.shape[0]
    flat = jnp.concatenate([s[..., 0].reshape(b, SSM_N), s[..., 1].reshape(b, SSM_N)], axis=1)
    return flat.reshape(b // SUBLANES, SUBLANES, 2 * SSM_N)


def _rows_to_state(r):
    b = r.shape[0] * SUBLANES
    flat = r.reshape(b, 2, SSM_G, SSM_P)
    return jnp.stack([flat[:, 0], flat[:, 1]], axis=-1)


def _merge_kernel(x_ref, g_ref, scale_ref, shift_ref, gate_ref, ya_ref, yb_ref, yc_ref, yd_ref,
                  wz_ref, wg_ref, wbr_ref, wout_ref, fg_ref, o_ref, *, last):
    x = x_ref[...]
    hb = _modulated_norm(x, g_ref[...], scale_ref[...], shift_ref[...]).astype(BF16)
    z = _dot(hb, wz_ref[...])
    merged = None
    for n, y_ref in enumerate((ya_ref, yb_ref, yc_ref, yd_ref)):
        zn = z[:, n * BRANCH_W:(n + 1) * BRANCH_W]
        yn = y_ref[...] * (zn * jax.nn.sigmoid(zn))
        t = _dot(yn.astype(BF16), wbr_ref[n * BRANCH_W:(n + 1) * BRANCH_W, :])
        gn = _dot(hb, wg_ref[:, n * D_MODEL:(n + 1) * D_MODEL])
        term = jax.nn.sigmoid(gn) * t
        merged = term if merged is None else merged + term
    out = x + gate_ref[...] * _dot(merged.astype(BF16), wout_ref[...])
    if last:
        out = (out * lax.rsqrt(jnp.mean(out * out, axis=-1, keepdims=True) + EPS)) * fg_ref[...]
    o_ref[...] = out


def _merge(x, mod_scale, mod_shift, mod_gate, ya, yb, yc3, yd, lw, final_g, *, batch, seq, tm, last, name):
    n = batch * seq
    tiles_per_seq = seq // tm
    per_batch_mod = mod_scale.shape[0] > 1

    def mod_idx(i):
        return ((i // tiles_per_seq) if per_batch_mod else 0, 0, 0)

    def yc_idx(i):
        b = i // tiles_per_seq
        return (b // SUBLANES, i % tiles_per_seq, b % SUBLANES)

    row = lambda i: (i, 0)
    ytile = pl.BlockSpec((tm, BRANCH_W), row)
    return pl.pallas_call(
        functools.partial(_merge_kernel, last=last),
        grid=(n // tm,),
        in_specs=[
            pl.BlockSpec((tm, D_MODEL), row),
            _full((1, D_MODEL)),
            pl.BlockSpec((None, 1, D_MODEL), mod_idx),
            pl.BlockSpec((None, 1, D_MODEL), mod_idx),
            pl.BlockSpec((None, 1, D_MODEL), mod_idx),
            ytile, ytile,
            pl.BlockSpec((None, tm, BRANCH_W), yc_idx),
            ytile,
            _full((D_MODEL, 4 * BRANCH_W)),
            _full((D_MODEL, 4 * D_MODEL)),
            _full((4 * BRANCH_W, D_MODEL)),
            _full((D_MODEL, D_MODEL)),
            _full((1, D_MODEL)),
        ],
        out_specs=pl.BlockSpec((tm, D_MODEL), row),
        out_shape=jax.ShapeDtypeStruct((n, D_MODEL), F32),
        compiler_params=_params("parallel"),
        name=name,
    )(x, lw["norm_g"], mod_scale, mod_shift, mod_gate, ya, yb, yc3, yd,
      lw["w_z"], lw["w_g"], lw["w_branch"], lw["w_out"], final_g)


def _rope_tables(seq):
    t = np.arange(seq)
    row = (t // GRID_W).astype(np.float32)
    col = (t % GRID_W).astype(np.float32)
    nf = HEAD_DIM // 4
    freqs = jnp.asarray(ROPE_BASE, F32) ** (-jnp.arange(nf, dtype=F32) / nf)
    ang_r = jnp.asarray(row)[:, None] * freqs[None, :]
    ang_c = jnp.asarray(col)[:, None] * freqs[None, :]
    cos = jnp.concatenate([jnp.cos(ang_r)] * 2 + [jnp.cos(ang_c)] * 2, axis=1)
    sin = jnp.concatenate([-jnp.sin(ang_r), jnp.sin(ang_r), -jnp.sin(ang_c), jnp.sin(ang_c)], axis=1)
    return jnp.tile(cos, (1, GQA_HEADS)), jnp.tile(sin, (1, GQA_HEADS))


def _layer_weights(l, norm_g, w_in, gm_v_g, gm_ws, gm_b, ssm_d, w_glu, b_glu, gqa_q_g, gqa_k_g, w_branch, w_out):
    w = w_in[l]
    w_proj = jnp.concatenate([w[:, 0:512], w[:, 768:1536], w[:, 1792:2048], w[:, 2304:2816]], axis=1).astype(BF16)
    w_z = jnp.concatenate([w[:, 512:768], w[:, 1536:1792], w[:, 2048:2304], w[:, 2816:3072]], axis=1).astype(BF16)
    return dict(
        norm_g=norm_g[l].reshape(1, D_MODEL),
        w_proj=w_proj,
        w_z=w_z,
        w_g=w[:, 3072:].astype(BF16),
        gm_v_g=gm_v_g[l].reshape(1, BRANCH_W),
        gm_w=gm_ws[l].reshape(GM_GROUPS * CHUNK, CHUNK).astype(BF16),
        gm_bias=jnp.repeat(gm_b[l].T, HEAD_DIM, axis=1),
        ssm_d=ssm_d[l].reshape(1, BRANCH_W),
        w_glu=w_glu[l].astype(BF16),
        b_glu=b_glu[l].reshape(1, BRANCH_W),
        gqa_q_g=jnp.tile(gqa_q_g[l], GQA_HEADS).reshape(1, BRANCH_W),
        gqa_k_g=jnp.tile(gqa_k_g[l], GQA_KV).reshape(1, 128),
        w_branch=w_branch[l].reshape(4 * BRANCH_W, D_MODEL).astype(BF16),
        w_out=w_out[l].astype(BF16),
    )


def _trunk_layer(x, mods, lw, sps, consts, final_g, cache, *, batch, seq, latent, last):
    scale, shift, gate = mods
    tm = 512 if latent else 256
    ya, bq, bk, bv, cx, dq, dk, dv = _inproj(x, scale, shift, lw, consts, batch=batch, seq=seq, tm=tm, latent=latent)
    halves = batch // SUBLANES
    cx3 = cx.reshape(halves, seq * SUBLANES, BRANCH_W)
    if latent:
        yb = _na_attention(bq, bk, bv, cache["na_k"], cache["na_v"], cache["na_bias"], batch=batch, seq=seq)
        yd = _block_attention(dq, dk, dv, cache["gqa_k"], cache["gqa_v"], batch=batch, seq=seq,
                              kvh=GQA_KV, rep=GQA_HEADS // GQA_KV, tq=256, ck=512, name="gqa_latent")
        s0 = cache["ssm"]
    else:
        yb = _block_attention(bq, bk, bv, None, None, batch=batch, seq=seq,
                              kvh=NA_HEADS, rep=1, tq=seq, ck=seq, name="na_ctx")
        yd = _block_attention(dq, dk, dv, None, None, batch=batch, seq=seq,
                              kvh=GQA_KV, rep=GQA_HEADS // GQA_KV, tq=seq, ck=seq, name="gqa_ctx")
        zero = jnp.zeros((halves, SUBLANES, 2 * SSM_N), F32)
        s0 = (zero, zero)
    steps = 64
    yf, sf = _s5_pass(cx3, sps[0], s0[0], None, lw, steps=steps, reverse=False)
    yc3, sr = _s5_pass(cx3, sps[1], s0[1], yf, lw, steps=steps, reverse=True)
    yc3 = yc3.reshape(halves, seq, SUBLANES * BRANCH_W)
    x_new = _merge(x, scale, shift, gate, ya, yb, yc3, yd, lw, final_g, batch=batch, seq=seq, tm=tm, last=last,
                   name="merge_latent" if latent else "merge_ctx")
    return x_new, (bk, bv, dk, dv, sf, sr)


def kernel(x_prompt, x_sample, c, cache_na_k, cache_na_v, cache_gqa_k, cache_gqa_v, state_ssm, c_ctx,
           norm_g, w_ada, b_ada, w_in, gm_v_g, gm_ws, gm_b, na_rel_bias, ssm_a_re, ssm_a_im, ssm_log_dt,
           ssm_b_re, ssm_b_im, ssm_c_re, ssm_c_im, ssm_d, w_glu, b_glu, gqa_q_g, gqa_k_g, w_branch, w_out,
           final_g):
    bc, lc, _ = x_prompt.shape
    bl, ll, _ = x_sample.shape
    past = cache_na_k.shape[2]

    n_rows = 16
    cond = jnp.zeros((n_rows, D_MODEL), F32).at[0].set(c_ctx).at[1:1 + bl].set(c)
    mod = _modulation(cond, w_ada, b_ada)

    def mods_of(l, lo, hi):
        m = mod[l, lo:hi].reshape(hi - lo, 1, 3 * D_MODEL)
        return m[..., 0:D_MODEL], m[..., D_MODEL:2 * D_MODEL], m[..., 2 * D_MODEL:]

    cos, sin = _rope_tables(ll)
    ones_blk = jnp.asarray(np.kron(np.eye(BRANCH_W // HEAD_DIM), np.ones((HEAD_DIM, HEAD_DIM))), BF16)
    consts = dict(rope_cos=cos, rope_sin=sin, ones_blk=ones_blk)
    fg = final_g.reshape(1, D_MODEL)

    xp = x_prompt.reshape(bc * lc, D_MODEL)
    xs = x_sample.reshape(bl * ll, D_MODEL)
    na_k_l, na_v_l, gqa_k_l, gqa_v_l, ssm_l = [], [], [], [], []
    for l in range(DEPTH):
        lw = _layer_weights(l, norm_g, w_in, gm_v_g, gm_ws, gm_b, ssm_d, w_glu, b_glu, gqa_q_g, gqa_k_g,
                            w_branch, w_out)
        sps = [_s5_discretise(ssm_a_re[l, d], ssm_a_im[l, d], ssm_log_dt[l, d], ssm_b_re[l, d], ssm_b_im[l, d],
                              ssm_c_re[l, d], ssm_c_im[l, d]) for d in range(2)]
        last = l == DEPTH - 1
        shift, scale, gate = mods_of(l, 0, 1)
        xp, (k_na, v_na, k_g, v_g, sf, sr) = _trunk_layer(
            xp, (scale, shift, gate), lw, sps, consts, fg, None, batch=bc, seq=lc, latent=False, last=last)
        na_k_l.append(k_na.reshape(bc, lc, NA_HEADS, HEAD_DIM))
        na_v_l.append(v_na.reshape(bc, lc, NA_HEADS, HEAD_DIM))
        gqa_k_l.append(k_g.reshape(bc, lc, GQA_KV, HEAD_DIM))
        gqa_v_l.append(v_g.reshape(bc, lc, GQA_KV, HEAD_DIM))
        ssm_l.append(jnp.stack([_rows_to_state(sf), _rows_to_state(sr)], axis=1))
        shift, scale, gate = mods_of(l, 1, 1 + bl)
        cache = dict(
            na_k=cache_na_k[:, l].reshape(bl, past, BRANCH_W).astype(BF16),
            na_v=cache_na_v[:, l].reshape(bl, past, BRANCH_W).astype(BF16),
            gqa_k=cache_gqa_k[:, l].reshape(bl, past, GQA_KV * HEAD_DIM).astype(BF16),
            gqa_v=cache_gqa_v[:, l].reshape(bl, past, GQA_KV * HEAD_DIM).astype(BF16),
            ssm=(_state_to_rows(state_ssm[:, l, 0]), _state_to_rows(state_ssm[:, l, 1])),
            na_bias=_na_bias_tables(na_rel_bias[l], ll // GRID_W),
        )
        xs, _ = _trunk_layer(xs, (scale, shift, gate), lw, sps, consts, fg, cache,
                             batch=bl, seq=ll, latent=True, last=last)
    return (xp.reshape(bc, lc, D_MODEL), xs.reshape(bl, ll, D_MODEL),
            jnp.stack(na_k_l, axis=1), jnp.stack(na_v_l, axis=1),
            jnp.stack(gqa_k_l, axis=1), jnp.stack(gqa_v_l, axis=1), jnp.stack(ssm_l, axis=1))
```

```python
import functools
import math

import numpy as np
import jax
import jax.numpy as jnp
from jax import lax
from jax.experimental import pallas as pl
from jax.experimental.pallas import tpu as pltpu

D_MODEL = 1024
DEPTH = 2
GRID_W = 64
BRANCH_W = 256
HEAD_DIM = 64
CHUNK = 128
GM_GROUPS = 4
NA_HEADS = 4
NA_KH = 8
NA_KW = 16
SSM_H = 16
SSM_G = 16
SSM_P = 64
SSM_N = SSM_G * SSM_P
GQA_HEADS = 4
GQA_KV = 2
ROPE_BASE = 10000.0
EPS = 1e-6
NEG_INF = -1e30
ATTN_SCALE = HEAD_DIM ** -0.5

SUBLANES = 8
VMEM_LIMIT = 56 * 1024 * 1024

F32 = jnp.float32
BF16 = jnp.bfloat16


def _params(*sem):
    return pltpu.CompilerParams(dimension_semantics=sem, vmem_limit_bytes=VMEM_LIMIT)


def _dot(a, b):
    return jnp.dot(a, b, preferred_element_type=F32)


def _dot_nt(a, b):
    return lax.dot_general(a, b, (((1,), (1,)), ((), ())), preferred_element_type=F32)


def _full(shape):
    nd = len(shape)
    return pl.BlockSpec(shape, lambda *_: (0,) * nd)


def _modulated_norm(x, g, scale, shift):
    y = x * lax.rsqrt(jnp.mean(x * x, axis=-1, keepdims=True) + EPS)
    return (y * g) * (1.0 + scale) + shift


def _group_sumsq(x, ones_blk):
    x2 = x * x
    hi = x2.astype(BF16)
    lo = (x2 - hi.astype(F32)).astype(BF16)
    return _dot(hi, ones_blk) + _dot(lo, ones_blk)


def _head_rmsnorm(x, ones_blk, g):
    ss = _group_sumsq(x, ones_blk)
    return (x * lax.rsqrt(ss * (1.0 / HEAD_DIM) + EPS)) * g


def _rope(x, cos, sin_signed):
    w = x.shape[1]
    lane = lax.broadcasted_iota(jnp.int32, x.shape, 1)
    up = pltpu.roll(x, 16, axis=1)
    dn = pltpu.roll(x, w - 16, axis=1)
    partner = jnp.where((lane & 16) != 0, up, dn)
    return x * cos + partner * sin_signed


def _mod_kernel(cond_ref, w_ref, b_ref, o_ref):
    c = cond_ref[...]
    s = (c * jax.nn.sigmoid(c)).astype(BF16)
    o_ref[...] = _dot(s, w_ref[...].astype(BF16)) + b_ref[...]


def _modulation(cond, w_ada, b_ada):
    r = cond.shape[0]
    nj = 3
    return pl.pallas_call(
        _mod_kernel,
        grid=(DEPTH, nj),
        in_specs=[
            pl.BlockSpec((r, D_MODEL), lambda l, j: (0, 0)),
            pl.BlockSpec((None, D_MODEL, D_MODEL), lambda l, j: (l, 0, j)),
            pl.BlockSpec((None, 1, D_MODEL), lambda l, j: (l, 0, j)),
        ],
        out_specs=pl.BlockSpec((None, r, D_MODEL), lambda l, j: (l, 0, j)),
        out_shape=jax.ShapeDtypeStruct((DEPTH, r, 3 * D_MODEL), F32),
        compiler_params=_params("arbitrary", "arbitrary"),
        name="adaln_modulation",
    )(cond, w_ada, b_ada.reshape(DEPTH, 1, 3 * D_MODEL))


def _inproj_kernel(x_ref, g_ref, scale_ref, shift_ref, w_ref, gmg_ref, gmw_ref, gmb_ref, ones_ref,
                   qg_ref, kg_ref, *rest, tm, latent):
    if latent:
        cos_ref, sin_ref = rest[:2]
        rest = rest[2:]
    ya_ref, bq_ref, bk_ref, bv_ref, cx_ref, dq_ref, dk_ref, dv_ref = rest

    h = _modulated_norm(x_ref[...], g_ref[...], scale_ref[...], shift_ref[...])
    p = _dot(h.astype(BF16), w_ref[...])

    u = p[:, 0:256]
    v = p[:, 256:512]
    vn = (v * lax.rsqrt(jnp.mean(v * v, axis=-1, keepdims=True) + EPS)) * gmg_ref[...]
    vnb = vn.astype(BF16)
    lane_grp = lax.broadcasted_iota(jnp.int32, (CHUNK, BRANCH_W), 1) // HEAD_DIM
    for c in range(tm // CHUNK):
        r = _dot(gmw_ref[...], vnb[c * CHUNK:(c + 1) * CHUNK, :])
        sp = r[0:CHUNK]
        for g in range(1, GM_GROUPS):
            sp = jnp.where(lane_grp == g, r[g * CHUNK:(g + 1) * CHUNK], sp)
        sp = sp + gmb_ref[...]
        ya_ref[c * CHUNK:(c + 1) * CHUNK, :] = u[c * CHUNK:(c + 1) * CHUNK, :] * sp

    bq = p[:, 512:768]
    if not latent:
        bq = bq * QK_SCALE_LOG2
    bq_ref[...] = bq.astype(bq_ref.dtype)
    bk_ref[...] = p[:, 768:1024].astype(bk_ref.dtype)
    bv_ref[...] = p[:, 1024:1280].astype(bv_ref.dtype)
    cx_ref[...] = p[:, 1280:1536]
    ones_blk = ones_ref[...]
    dq = _head_rmsnorm(p[:, 1536:1792], ones_blk, qg_ref[...])
    dk = _head_rmsnorm(p[:, 1792:1920], ones_blk[0:128, 0:128], kg_ref[...])
    if latent:
        cos = cos_ref[...]
        sin = sin_ref[...]
        dq = _rope(dq, cos, sin)
        dk = _rope(dk, cos[:, 0:128], sin[:, 0:128])
    dq_ref[...] = (dq * QK_SCALE_LOG2).astype(dq_ref.dtype)
    dk_ref[...] = dk.astype(dk_ref.dtype)
    dv_ref[...] = p[:, 1920:2048].astype(dv_ref.dtype)


def _inproj(x, mod_scale, mod_shift, lw, consts, *, batch, seq, tm, latent):
    n = batch * seq
    tiles_per_seq = seq // tm
    halves = batch // SUBLANES
    per_batch_mod = mod_scale.shape[0] > 1

    def mod_idx(i):
        return ((i // tiles_per_seq) if per_batch_mod else 0, 0, 0)

    row = lambda i: (i, 0)
    kv_dtype = BF16 if latent else F32
    in_specs = [
        pl.BlockSpec((tm, D_MODEL), row),
        _full((1, D_MODEL)),
        pl.BlockSpec((None, 1, D_MODEL), mod_idx),
        pl.BlockSpec((None, 1, D_MODEL), mod_idx),
        _full((D_MODEL, 2048)),
        _full((1, BRANCH_W)),
        _full((GM_GROUPS * CHUNK, CHUNK)),
        _full((CHUNK, BRANCH_W)),
        _full((BRANCH_W, BRANCH_W)),
        _full((1, BRANCH_W)),
        _full((1, 128)),
    ]
    args = [x, lw["norm_g"], mod_scale, mod_shift, lw["w_proj"], lw["gm_v_g"], lw["gm_w"], lw["gm_bias"],
            consts["ones_blk"], lw["gqa_q_g"], lw["gqa_k_g"]]
    if latent:
        rope_idx = lambda i: (i % tiles_per_seq, 0)
        in_specs += [pl.BlockSpec((tm, BRANCH_W), rope_idx), pl.BlockSpec((tm, BRANCH_W), rope_idx)]
        args += [consts["rope_cos"], consts["rope_sin"]]

    def cx_idx(i):
        b = i // tiles_per_seq
        return (b // SUBLANES, i % tiles_per_seq, b % SUBLANES)

    out_specs = [
        pl.BlockSpec((tm, BRANCH_W), row),
        pl.BlockSpec((tm, BRANCH_W), row),
        pl.BlockSpec((tm, BRANCH_W), row),
        pl.BlockSpec((tm, BRANCH_W), row),
        pl.BlockSpec((None, tm, BRANCH_W), cx_idx),
        pl.BlockSpec((tm, BRANCH_W), row),
        pl.BlockSpec((tm, 128), row),
        pl.BlockSpec((tm, 128), row),
    ]
    out_shape = [
        jax.ShapeDtypeStruct((n, BRANCH_W), F32),
        jax.ShapeDtypeStruct((n, BRANCH_W), BF16),
        jax.ShapeDtypeStruct((n, BRANCH_W), kv_dtype),
        jax.ShapeDtypeStruct((n, BRANCH_W), kv_dtype),
        jax.ShapeDtypeStruct((halves, seq, SUBLANES * BRANCH_W), F32),
        jax.ShapeDtypeStruct((n, BRANCH_W), BF16),
        jax.ShapeDtypeStruct((n, 128), kv_dtype),
        jax.ShapeDtypeStruct((n, 128), kv_dtype),
    ]
    return pl.pallas_call(
        functools.partial(_inproj_kernel, tm=tm, latent=latent),
        grid=(n // tm,),
        in_specs=in_specs,
        out_specs=out_specs,
        out_shape=out_shape,
        compiler_params=_params("parallel"),
        name="inproj_latent" if latent else "inproj_ctx",
    )(*args)


VT_ROWS = HEAD_DIM + 16
QK_SCALE_LOG2 = ATTN_SCALE * math.log2(math.e)


def _block_attn_kernel(q_ref, k_ref, vt_ref, *rest, tq, kvh, rep, lk, ck, cache_len):
    if cache_len:
        ck_ref, cvt_ref, o_ref, s_scr = rest
    else:
        o_ref, s_scr = rest
    steps = []
    for j in range(kvh):
        steps += [(j, c, False) for c in range(lk // ck)]
        if cache_len:
            steps.append((j, 0, True))
    per_head = len(steps) // kvh
    qs = [jnp.concatenate([q_ref[:, (j * rep + g) * HEAD_DIM:(j * rep + g + 1) * HEAD_DIM] for g in range(rep)],
                          axis=0) for j in range(kvh)]

    def logits(t):
        j, c, cached = steps[t]
        lanes = slice(j * HEAD_DIM, (j + 1) * HEAD_DIM)
        kc = ck_ref[:, lanes] if cached else k_ref[c * ck:(c + 1) * ck, lanes]
        st = _dot_nt(kc.astype(BF16), qs[j])
        s_scr[t % 2, 0:st.shape[0], :] = st
        return jnp.max(st, axis=0, keepdims=True)

    cmax = logits(0)
    m = acc = None
    for t, (j, c, cached) in enumerate(steps):
        nxt = logits(t + 1) if t + 1 < len(steps) else None
        n_keys = cache_len if cached else ck
        vrows = slice(j * VT_ROWS, (j + 1) * VT_ROWS)
        vt = cvt_ref[vrows, :] if cached else vt_ref[vrows, c * ck:(c + 1) * ck]
        st = s_scr[t % 2, 0:n_keys, :]
        if t % per_head == 0:
            m = cmax
            acc = _dot(vt, jnp.exp2(st - m).astype(BF16))
        else:
            m_new = jnp.maximum(m, cmax)
            acc = jnp.exp2(m - m_new) * acc + _dot(vt, jnp.exp2(st - m_new).astype(BF16))
            m = m_new
        if t % per_head == per_head - 1:
            o = (acc[0:HEAD_DIM] / acc[HEAD_DIM:HEAD_DIM + 1]).T
            for g in range(rep):
                h = j * rep + g
                o_ref[:, h * HEAD_DIM:(h + 1) * HEAD_DIM] = o[g * tq:(g + 1) * tq, :]
        cmax = nxt


def _vt_layout(v, batch, seq, kvh):
    vt = v.reshape(batch, seq, kvh, HEAD_DIM).transpose(0, 2, 3, 1).astype(BF16)
    ones = jnp.ones((batch, kvh, VT_ROWS - HEAD_DIM, seq), BF16)
    return jnp.concatenate([vt, ones], axis=2).reshape(batch, kvh * VT_ROWS, seq)


def _block_attention(q, k, v, cache_k, cache_v, *, batch, seq, kvh, rep, tq, ck, name):
    n = batch * seq
    wkv = kvh * HEAD_DIM
    nq = seq // tq
    cache_len = 0 if cache_k is None else cache_k.shape[1]
    in_specs = [
        pl.BlockSpec((tq, BRANCH_W), lambda b, i: (b * nq + i, 0)),
        pl.BlockSpec((None, seq, wkv), lambda b, i: (b, 0, 0)),
        pl.BlockSpec((None, kvh * VT_ROWS, seq), lambda b, i: (b, 0, 0)),
    ]
    args = [q, k.reshape(batch, seq, wkv), _vt_layout(v, batch, seq, kvh)]
    if cache_len:
        in_specs += [pl.BlockSpec((None, cache_len, wkv), lambda b, i: (b, 0, 0)),
                     pl.BlockSpec((None, kvh * VT_ROWS, cache_len), lambda b, i: (b, 0, 0))]
        args += [cache_k, _vt_layout(cache_v.reshape(batch * cache_len, wkv), batch, cache_len, kvh)]
    return pl.pallas_call(
        functools.partial(_block_attn_kernel, tq=tq, kvh=kvh, rep=rep, lk=seq, ck=ck, cache_len=cache_len),
        grid=(batch, nq),
        in_specs=in_specs,
        out_specs=pl.BlockSpec((tq, BRANCH_W), lambda b, i: (b * nq + i, 0)),
        out_shape=jax.ShapeDtypeStruct((n, BRANCH_W), F32),
        scratch_shapes=[pltpu.VMEM((2, max(ck, cache_len), rep * tq), F32)],
        compiler_params=_params("parallel", "arbitrary"),
        name=name,
    )(*args)


NA_QROWS = 4
NA_KROWS = NA_QROWS + NA_KH - 1
NA_TQ = NA_QROWS * GRID_W
NA_TK = NA_KROWS * GRID_W


def _na_kernel(q_ref, k_ref, v_ref, ck_ref, cv_ref, bias_ref, o_ref, *, rows):
    i = pl.program_id(1)
    kbase = jnp.clip(i * NA_QROWS - NA_KH // 2, 0, rows - NA_KROWS)
    r0 = pl.multiple_of(kbase * GRID_W, GRID_W)
    for h in range(NA_HEADS):
        lanes = slice(h * HEAD_DIM, (h + 1) * HEAD_DIM)
        qh = q_ref[:, lanes] * ATTN_SCALE
        s_win = _dot_nt(qh, k_ref[pl.ds(r0, NA_TK), lanes]) + bias_ref[h]
        s_ctx = _dot_nt(qh, ck_ref[:, lanes])
        m = jnp.maximum(jnp.max(s_win, axis=-1, keepdims=True), jnp.max(s_ctx, axis=-1, keepdims=True))
        p_win = jnp.exp(s_win - m)
        p_ctx = jnp.exp(s_ctx - m)
        l = jnp.sum(p_win, axis=-1, keepdims=True) + jnp.sum(p_ctx, axis=-1, keepdims=True)
        acc = _dot(p_win.astype(BF16), v_ref[pl.ds(r0, NA_TK), lanes]) + _dot(p_ctx.astype(BF16), cv_ref[:, lanes])
        o_ref[:, lanes] = acc / l


def _na_attention(q, k, v, cache_k, cache_v, bias, *, batch, seq):
    n = batch * seq
    rows = seq // GRID_W
    nq = rows // NA_QROWS
    cache_len = cache_k.shape[1]

    def bias_idx(b, i):
        return (jnp.where(i == 0, 0, jnp.where(i == nq - 1, 2, 1)), 0, 0, 0)

    return pl.pallas_call(
        functools.partial(_na_kernel, rows=rows),
        grid=(batch, nq),
        in_specs=[
            pl.BlockSpec((NA_TQ, BRANCH_W), lambda b, i: (b * nq + i, 0)),
            pl.BlockSpec((None, seq, BRANCH_W), lambda b, i: (b, 0, 0)),
            pl.BlockSpec((None, seq, BRANCH_W), lambda b, i: (b, 0, 0)),
            pl.BlockSpec((None, cache_len, BRANCH_W), lambda b, i: (b, 0, 0)),
            pl.BlockSpec((None, cache_len, BRANCH_W), lambda b, i: (b, 0, 0)),
            pl.BlockSpec((None, NA_HEADS, NA_TQ, NA_TK), bias_idx),
        ],
        out_specs=pl.BlockSpec((NA_TQ, BRANCH_W), lambda b, i: (b * nq + i, 0)),
        out_shape=jax.ShapeDtypeStruct((n, BRANCH_W), F32),
        compiler_params=_params("parallel", "arbitrary"),
        name="na_latent",
    )(q, k.reshape(batch, seq, BRANCH_W), v.reshape(batch, seq, BRANCH_W), cache_k, cache_v, bias)


def _na_bias_tables(rel_bias, rows):
    nq = rows // NA_QROWS
    kh = min(NA_KH, rows)
    n_dr, n_dc = 2 * NA_KH - 1, 2 * NA_KW - 1
    cq = np.arange(GRID_W)
    start_c = np.clip(cq - NA_KW // 2, 0, GRID_W - NA_KW)
    col_ok = (cq[None, :] >= start_c[:, None]) & (cq[None, :] < start_c[:, None] + NA_KW)
    dc = np.clip(cq[None, :] - cq[:, None], -(NA_KW - 1), NA_KW - 1) + (NA_KW - 1)
    onehot = (dc.reshape(1, -1) == np.arange(n_dc)[:, None]).astype(np.float32)
    per_dr = jnp.einsum("hrc,cx->hrx", rel_bias.astype(F32), jnp.asarray(onehot), precision=lax.Precision.HIGHEST)
    per_dr = jnp.where(col_ok.reshape(1, 1, -1), per_dr, NEG_INF)
    masked = jnp.full((NA_HEADS, 1, GRID_W * GRID_W), NEG_INF, F32)
    per_dr = jnp.concatenate([per_dr, masked], axis=1)
    tabs = []
    for blk in (0, 1, nq - 1):
        kbase = int(np.clip(blk * NA_QROWS - NA_KH // 2, 0, rows - NA_KROWS))
        r = blk * NA_QROWS + np.arange(NA_QROWS)
        start_r = np.clip(r - kh // 2, 0, rows - kh)
        kr = kbase + np.arange(NA_KROWS)
        row_ok = (kr[None, :] >= start_r[:, None]) & (kr[None, :] < start_r[:, None] + kh)
        dr = np.where(row_ok, kr[None, :] - r[:, None] + (NA_KH - 1), n_dr)
        t = jnp.stack([jnp.stack([per_dr[:, int(dr[a, b])] for b in range(NA_KROWS)], axis=1)
                       for a in range(NA_QROWS)], axis=1)
        t = t.reshape(NA_HEADS, NA_QROWS, NA_KROWS, GRID_W, GRID_W).transpose(0, 1, 3, 2, 4)
        tabs.append(t.reshape(NA_HEADS, NA_TQ, NA_TK))
    return jnp.stack(tabs, axis=0)


def _gelu_tanh(y):
    return 0.5 * y * (1.0 + jnp.tanh(math.sqrt(2.0 / math.pi) * (y + 0.044715 * (y * y * y))))


def _s5_kernel(x_ref, bmat_ref, cmat_ref, abar_ref, s0_ref, *rest, steps, reverse):
    if reverse:
        yprev_ref, dvec_ref, wglu_ref, bglu_ref, y_ref, sfin_ref, bu_ref, st_ref = rest
    else:
        y_ref, sfin_ref, bu_ref, st_ref = rest
    j = pl.program_id(1)

    @pl.when(j == 0)
    def _():
        st_ref[...] = s0_ref[...]

    x = x_ref[...]
    bu_ref[...] = _dot(x.astype(BF16), bmat_ref[...])
    a_re = jnp.broadcast_to(abar_ref[:, 0:SSM_N], (SUBLANES, SSM_N))
    a_im = jnp.broadcast_to(abar_ref[:, SSM_N:], (SUBLANES, SSM_N))

    def step(i, carry):
        s_re, s_im = carry
        t = (steps - 1 - i) if reverse else i
        r0 = pl.multiple_of(t * SUBLANES, SUBLANES)
        n_re = a_re * s_re - a_im * s_im + bu_ref[pl.ds(r0, SUBLANES), 0:SSM_N]
        n_im = a_re * s_im + a_im * s_re + bu_ref[pl.ds(r0, SUBLANES), SSM_N:]
        bu_ref[pl.ds(r0, SUBLANES), 0:SSM_N] = n_re
        bu_ref[pl.ds(r0, SUBLANES), SSM_N:] = n_im
        return n_re, n_im

    s_re, s_im = lax.fori_loop(0, steps, step, (st_ref[:, 0:SSM_N], st_ref[:, SSM_N:]))
    st_ref[:, 0:SSM_N] = s_re
    st_ref[:, SSM_N:] = s_im
    sfin_ref[...] = st_ref[...]

    y = _dot(bu_ref[...].astype(BF16), cmat_ref[...])
    if reverse:
        y = yprev_ref[...] + y + dvec_ref[...] * x
        y = _gelu_tanh(y)
        y = y * jax.nn.sigmoid(_dot(y.astype(BF16), wglu_ref[...]) + bglu_ref[...])
    y_ref[...] = y


def _s5_pass(x3, sp, s0, yprev, lw, *, steps, reverse):
    halves, rows, _ = x3.shape
    nchunk = rows // (steps * SUBLANES)
    tr = steps * SUBLANES

    def blk(h, j):
        return (h, (nchunk - 1 - j) if reverse else j, 0)

    in_specs = [
        pl.BlockSpec((None, tr, BRANCH_W), blk),
        _full((BRANCH_W, 2 * SSM_N)),
        _full((2 * SSM_N, BRANCH_W)),
        _full((1, 2 * SSM_N)),
        pl.BlockSpec((None, SUBLANES, 2 * SSM_N), lambda h, j: (h, 0, 0)),
    ]
    args = [x3, sp["bmat"], sp["cmat"], sp["abar"], s0]
    if reverse:
        in_specs += [pl.BlockSpec((None, tr, BRANCH_W), blk), _full((1, BRANCH_W)),
                     _full((BRANCH_W, BRANCH_W)), _full((1, BRANCH_W))]
        args += [yprev, lw["ssm_d"], lw["w_glu"], lw["b_glu"]]
    return pl.pallas_call(
        functools.partial(_s5_kernel, steps=steps, reverse=reverse),
        grid=(halves, nchunk),
        in_specs=in_specs,
        out_specs=[pl.BlockSpec((None, tr, BRANCH_W), blk),
                   pl.BlockSpec((None, SUBLANES, 2 * SSM_N), lambda h, j: (h, 0, 0))],
        out_shape=[jax.ShapeDtypeStruct(x3.shape, F32),
                   jax.ShapeDtypeStruct((halves, SUBLANES, 2 * SSM_N), F32)],
        scratch_shapes=[pltpu.VMEM((tr, 2 * SSM_N), F32), pltpu.VMEM((SUBLANES, 2 * SSM_N), F32)],
        compiler_params=_params("parallel", "arbitrary"),
        name="s5_reverse" if reverse else "s5_forward",
    )(*args)


def _s5_discretise(a_re, a_im, log_dt, b_re, b_im, c_re, c_im):
    dt = jnp.exp(log_dt)[:, None]
    mag = jnp.exp(dt * a_re)
    ab_re, ab_im = mag * jnp.cos(dt * a_im), mag * jnp.sin(dt * a_im)
    den = a_re * a_re + a_im * a_im
    nr, ni = ab_re - 1.0, ab_im
    coef_re = ((nr * a_re + ni * a_im) / den)[..., None]
    coef_im = ((ni * a_re - nr * a_im) / den)[..., None]
    bb_re = coef_re * b_re - coef_im * b_im
    bb_im = coef_re * b_im + coef_im * b_re
    eye = jnp.eye(SSM_G, dtype=F32)

    def in_blocks(bb):
        return jnp.einsum("gpi,gk->gikp", bb, eye).reshape(SSM_G * SSM_H, SSM_N)

    def out_blocks(cc):
        return jnp.einsum("gip,gk->gpki", cc, eye).reshape(SSM_N, SSM_G * SSM_H)

    bmat = jnp.concatenate([in_blocks(bb_re), in_blocks(bb_im)], axis=1).astype(BF16)
    cmat = jnp.concatenate([out_blocks(c_re), -out_blocks(c_im)], axis=0).astype(BF16)
    abar = jnp.concatenate([ab_re.reshape(1, SSM_N), ab_im.reshape(1, SSM_N)], axis=1)
    return dict(bmat=bmat, cmat=cmat, abar=abar)


def _state_to_rows(s):
    b = s.shape[0]
    flat = jnp.concatenate([s[..., 0].reshape(b, SSM_N), s[..., 1].reshape(b, SSM_N)], axis=1)
    return flat.reshape(b // SUBLANES, SUBLANES, 2 * SSM_N)


def _rows_to_state(r):
    b = r.shape[0] * SUBLANES
    flat = r.reshape(b, 2, SSM_G, SSM_P)
    return jnp.stack([flat[:, 0], flat[:, 1]], axis=-1)


def _merge_kernel(x_ref, g_ref, scale_ref, shift_ref, gate_ref, ya_ref, yb_ref, yc_ref, yd_ref,
                  wz_ref, wg_ref, wbr_ref, wout_ref, fg_ref, o_ref, *, last):
    x = x_ref[...]
    hb = _modulated_norm(x, g_ref[...], scale_ref[...], shift_ref[...]).astype(BF16)
    z = _dot(hb, wz_ref[...])
    merged = None
    for n, y_ref in enumerate((ya_ref, yb_ref, yc_ref, yd_ref)):
        zn = z[:, n * BRANCH_W:(n + 1) * BRANCH_W]
        yn = y_ref[...] * (zn * jax.nn.sigmoid(zn))
        t = _dot(yn.astype(BF16), wbr_ref[n * BRANCH_W:(n + 1) * BRANCH_W, :])
        gn = _dot(hb, wg_ref[:, n * D_MODEL:(n + 1) * D_MODEL])
        term = jax.nn.sigmoid(gn) * t
        merged = term if merged is None else merged + term
    out = x + gate_ref[...] * _dot(merged.astype(BF16), wout_ref[...])
    if last:
        out = (out * lax.rsqrt(jnp.mean(out * out, axis=-1, keepdims=True) + EPS)) * fg_ref[...]
    o_ref[...] = out


def _merge(x, mod_scale, mod_shift, mod_gate, ya, yb, yc3, yd, lw, final_g, *, batch, seq, tm, last, name):
    n = batch * seq
    tiles_per_seq = seq // tm
    per_batch_mod = mod_scale.shape[0] > 1

    def mod_idx(i):
        return ((i // tiles_per_seq) if per_batch_mod else 0, 0, 0)

    def yc_idx(i):
        b = i // tiles_per_seq
        return (b // SUBLANES, i % tiles_per_seq, b % SUBLANES)

    row = lambda i: (i, 0)
    ytile = pl.BlockSpec((tm, BRANCH_W), row)
    return pl.pallas_call(
        functools.partial(_merge_kernel, last=last),
        grid=(n // tm,),
        in_specs=[
            pl.BlockSpec((tm, D_MODEL), row),
            _full((1, D_MODEL)),
            pl.BlockSpec((None, 1, D_MODEL), mod_idx),
            pl.BlockSpec((None, 1, D_MODEL), mod_idx),
            pl.BlockSpec((None, 1, D_MODEL), mod_idx),
            ytile, ytile,
            pl.BlockSpec((None, tm, BRANCH_W), yc_idx),
            ytile,
            _full((D_MODEL, 4 * BRANCH_W)),
            _full((D_MODEL, 4 * D_MODEL)),
            _full((4 * BRANCH_W, D_MODEL)),
            _full((D_MODEL, D_MODEL)),
            _full((1, D_MODEL)),
        ],
        out_specs=pl.BlockSpec((tm, D_MODEL), row),
        out_shape=jax.ShapeDtypeStruct((n, D_MODEL), F32),
        compiler_params=_params("parallel"),
        name=name,
    )(x, lw["norm_g"], mod_scale, mod_shift, mod_gate, ya, yb, yc3, yd,
      lw["w_z"], lw["w_g"], lw["w_branch"], lw["w_out"], final_g)


def _rope_tables(seq):
    t = np.arange(seq)
    row = (t // GRID_W).astype(np.float32)
    col = (t % GRID_W).astype(np.float32)
    nf = HEAD_DIM // 4
    freqs = jnp.asarray(ROPE_BASE, F32) ** (-jnp.arange(nf, dtype=F32) / nf)
    ang_r = jnp.asarray(row)[:, None] * freqs[None, :]
    ang_c = jnp.asarray(col)[:, None] * freqs[None, :]
    cos = jnp.concatenate([jnp.cos(ang_r)] * 2 + [jnp.cos(ang_c)] * 2, axis=1)
    sin = jnp.concatenate([-jnp.sin(ang_r), jnp.sin(ang_r), -jnp.sin(ang_c), jnp.sin(ang_c)], axis=1)
    return jnp.tile(cos, (1, GQA_HEADS)), jnp.tile(sin, (1, GQA_HEADS))


def _layer_weights(l, norm_g, w_in, gm_v_g, gm_ws, gm_b, ssm_d, w_glu, b_glu, gqa_q_g, gqa_k_g, w_branch, w_out):
    w = w_in[l]
    w_proj = jnp.concatenate([w[:, 0:512], w[:, 768:1536], w[:, 1792:2048], w[:, 2304:2816]], axis=1).astype(BF16)
    w_z = jnp.concatenate([w[:, 512:768], w[:, 1536:1792], w[:, 2048:2304], w[:, 2816:3072]], axis=1).astype(BF16)
    return dict(
        norm_g=norm_g[l].reshape(1, D_MODEL),
        w_proj=w_proj,
        w_z=w_z,
        w_g=w[:, 3072:].astype(BF16),
        gm_v_g=gm_v_g[l].reshape(1, BRANCH_W),
        gm_w=gm_ws[l].reshape(GM_GROUPS * CHUNK, CHUNK).astype(BF16),
        gm_bias=jnp.repeat(gm_b[l].T, HEAD_DIM, axis=1),
        ssm_d=ssm_d[l].reshape(1, BRANCH_W),
        w_glu=w_glu[l].astype(BF16),
        b_glu=b_glu[l].reshape(1, BRANCH_W),
        gqa_q_g=jnp.tile(gqa_q_g[l], GQA_HEADS).reshape(1, BRANCH_W),
        gqa_k_g=jnp.tile(gqa_k_g[l], GQA_KV).reshape(1, 128),
        w_branch=w_branch[l].reshape(4 * BRANCH_W, D_MODEL).astype(BF16),
        w_out=w_out[l].astype(BF16),
    )


def _trunk_layer(x, mods, lw, sps, consts, final_g, cache, *, batch, seq, latent, last):
    scale, shift, gate = mods
    tm = 512 if latent else 256
    ya, bq, bk, bv, cx, dq, dk, dv = _inproj(x, scale, shift, lw, consts, batch=batch, seq=seq, tm=tm, latent=latent)
    halves = batch // SUBLANES
    cx3 = cx.reshape(halves, seq * SUBLANES, BRANCH_W)
    if latent:
        yb = _na_attention(bq, bk, bv, cache["na_k"], cache["na_v"], cache["na_bias"], batch=batch, seq=seq)
        yd = _block_attention(dq, dk, dv, cache["gqa_k"], cache["gqa_v"], batch=batch, seq=seq,
                              kvh=GQA_KV, rep=GQA_HEADS // GQA_KV, tq=256, ck=512, name="gqa_latent")
        s0 = cache["ssm"]
    else:
        yb = _block_attention(bq, bk, bv, None, None, batch=batch, seq=seq,
                              kvh=NA_HEADS, rep=1, tq=seq, ck=seq, name="na_ctx")
        yd = _block_attention(dq, dk, dv, None, None, batch=batch, seq=seq,
                              kvh=GQA_KV, rep=GQA_HEADS // GQA_KV, tq=seq, ck=seq, name="gqa_ctx")
        zero = jnp.zeros((halves, SUBLANES, 2 * SSM_N), F32)
        s0 = (zero, zero)
    steps = 64
    yf, sf = _s5_pass(cx3, sps[0], s0[0], None, lw, steps=steps, reverse=False)
    yc3, sr = _s5_pass(cx3, sps[1], s0[1], yf, lw, steps=steps, reverse=True)
    yc3 = yc3.reshape(halves, seq, SUBLANES * BRANCH_W)
    x_new = _merge(x, scale, shift, gate, ya, yb, yc3, yd, lw, final_g, batch=batch, seq=seq, tm=tm, last=last,
                   name="merge_latent" if latent else "merge_ctx")
    return x_new, (bk, bv, dk, dv, sf, sr)


def kernel(x_prompt, x_sample, c, cache_na_k, cache_na_v, cache_gqa_k, cache_gqa_v, state_ssm, c_ctx,
           norm_g, w_ada, b_ada, w_in, gm_v_g, gm_ws, gm_b, na_rel_bias, ssm_a_re, ssm_a_im, ssm_log_dt,
           ssm_b_re, ssm_b_im, ssm_c_re, ssm_c_im, ssm_d, w_glu, b_glu, gqa_q_g, gqa_k_g, w_branch, w_out,
           final_g):
    bc, lc, _ = x_prompt.shape
    bl, ll, _ = x_sample.shape
    past = cache_na_k.shape[2]

    n_rows = 16
    cond = jnp.zeros((n_rows, D_MODEL), F32).at[0].set(c_ctx).at[1:1 + bl].set(c)
    mod = _modulation(cond, w_ada, b_ada)

    def mods_of(l, lo, hi):
        m = mod[l, lo:hi].reshape(hi - lo, 1, 3 * D_MODEL)
        return m[..., 0:D_MODEL], m[..., D_MODEL:2 * D_MODEL], m[..., 2 * D_MODEL:]

    cos, sin = _rope_tables(ll)
    ones_blk = jnp.asarray(np.kron(np.eye(BRANCH_W // HEAD_DIM), np.ones((HEAD_DIM, HEAD_DIM))), BF16)
    consts = dict(rope_cos=cos, rope_sin=sin, ones_blk=ones_blk)
    fg = final_g.reshape(1, D_MODEL)

    xp = x_prompt.reshape(bc * lc, D_MODEL)
    xs = x_sample.reshape(bl * ll, D_MODEL)
    na_k_l, na_v_l, gqa_k_l, gqa_v_l, ssm_l = [], [], [], [], []
    for l in range(DEPTH):
        lw = _layer_weights(l, norm_g, w_in, gm_v_g, gm_ws, gm_b, ssm_d, w_glu, b_glu, gqa_q_g, gqa_k_g,
                            w_branch, w_out)
        sps = [_s5_discretise(ssm_a_re[l, d], ssm_a_im[l, d], ssm_log_dt[l, d], ssm_b_re[l, d], ssm_b_im[l, d],
                              ssm_c_re[l, d], ssm_c_im[l, d]) for d in range(2)]
        last = l == DEPTH - 1
        shift, scale, gate = mods_of(l, 0, 1)
        xp, (k_na, v_na, k_g, v_g, sf, sr) = _trunk_layer(
            xp, (scale, shift, gate), lw, sps, consts, fg, None, batch=bc, seq=lc, latent=False, last=last)
        na_k_l.append(k_na.reshape(bc, lc, NA_HEADS, HEAD_DIM))
        na_v_l.append(v_na.reshape(bc, lc, NA_HEADS, HEAD_DIM))
        gqa_k_l.append(k_g.reshape(bc, lc, GQA_KV, HEAD_DIM))
        gqa_v_l.append(v_g.reshape(bc, lc, GQA_KV, HEAD_DIM))
        ssm_l.append(jnp.stack([_rows_to_state(sf), _rows_to_state(sr)], axis=1))
        shift, scale, gate = mods_of(l, 1, 1 + bl)
        cache = dict(
            na_k=cache_na_k[:, l].reshape(bl, past, BRANCH_W).astype(BF16),
            na_v=cache_na_v[:, l].reshape(bl, past, BRANCH_W).astype(BF16),
            gqa_k=cache_gqa_k[:, l].reshape(bl, past, GQA_KV * HEAD_DIM).astype(BF16),
            gqa_v=cache_gqa_v[:, l].reshape(bl, past, GQA_KV * HEAD_DIM).astype(BF16),
            ssm=(_state_to_rows(state_ssm[:, l, 0]), _state_to_rows(state_ssm[:, l, 1])),
            na_bias=_na_bias_tables(na_rel_bias[l], ll // GRID_W),
        )
        xs, _ = _trunk_layer(xs, (scale, shift, gate), lw, sps, consts, fg, cache,
                             batch=bl, seq=ll, latent=True, last=last)
    return (xp.reshape(bc, lc, D_MODEL), xs.reshape(bl, ll, D_MODEL),
            jnp.stack(na_k_l, axis=1), jnp.stack(na_v_l, axis=1),
            jnp.stack(gqa_k_l, axis=1), jnp.stack(gqa_v_l, axis=1), jnp.stack(ssm_l, axis=1))
```

```python
import functools
import math

import numpy as np
import jax
import jax.numpy as jnp
from jax import lax
from jax.experimental import pallas as pl
from jax.experimental.pallas import tpu as pltpu

D_MODEL = 1024
DEPTH = 2
GRID_W = 64
BRANCH_W = 256
HEAD_DIM = 64
CHUNK = 128
GM_GROUPS = 4
NA_HEADS = 4
NA_KH = 8
NA_KW = 16
SSM_H = 16
SSM_G = 16
SSM_P = 64
SSM_N = SSM_G * SSM_P
GQA_HEADS = 4
GQA_KV = 2
ROPE_BASE = 10000.0
EPS = 1e-6
NEG_INF = -1e30
ATTN_SCALE = HEAD_DIM ** -0.5

SUBLANES = 8
VMEM_LIMIT = 56 * 1024 * 1024

F32 = jnp.float32
BF16 = jnp.bfloat16


def _params(*sem):
    return pltpu.CompilerParams(dimension_semantics=sem, vmem_limit_bytes=VMEM_LIMIT)


def _dot(a, b):
    return jnp.dot(a, b, preferred_element_type=F32)


def _dot_nt(a, b):
    return lax.dot_general(a, b, (((1,), (1,)), ((), ())), preferred_element_type=F32)


def _full(shape):
    nd = len(shape)
    return pl.BlockSpec(shape, lambda *_: (0,) * nd)


def _modulated_norm(x, g, scale, shift):
    y = x * lax.rsqrt(jnp.mean(x * x, axis=-1, keepdims=True) + EPS)
    return (y * g) * (1.0 + scale) + shift


def _group_sumsq(x, ones_blk):
    x2 = x * x
    hi = x2.astype(BF16)
    lo = (x2 - hi.astype(F32)).astype(BF16)
    return _dot(hi, ones_blk) + _dot(lo, ones_blk)


def _head_rmsnorm(x, ones_blk, g):
    ss = _group_sumsq(x, ones_blk)
    return (x * lax.rsqrt(ss * (1.0 / HEAD_DIM) + EPS)) * g


def _rope(x, cos, sin_signed):
    w = x.shape[1]
    lane = lax.broadcasted_iota(jnp.int32, x.shape, 1)
    up = pltpu.roll(x, 16, axis=1)
    dn = pltpu.roll(x, w - 16, axis=1)
    partner = jnp.where((lane & 16) != 0, up, dn)
    return x * cos + partner * sin_signed


VT_ROWS = HEAD_DIM + 16
QK_SCALE_LOG2 = ATTN_SCALE * math.log2(math.e)


def _vt_tile(v, heads):
    vt = v.T.astype(BF16)
    ones = jnp.ones((VT_ROWS - HEAD_DIM, v.shape[0]), BF16)
    parts = []
    for h in range(heads):
        parts += [vt[h * HEAD_DIM:(h + 1) * HEAD_DIM], ones]
    return jnp.concatenate(parts, axis=0)


def _mod_kernel(cond_ref, w_ref, b_ref, o_ref):
    c = cond_ref[...]
    s = (c * jax.nn.sigmoid(c)).astype(BF16)
    o_ref[...] = _dot(s, w_ref[...].astype(BF16)) + b_ref[...]


def _modulation(cond, w_ada, b_ada):
    r = cond.shape[0]
    nj = 3
    return pl.pallas_call(
        _mod_kernel,
        grid=(DEPTH, nj),
        in_specs=[
            pl.BlockSpec((r, D_MODEL), lambda l, j: (0, 0)),
            pl.BlockSpec((None, D_MODEL, D_MODEL), lambda l, j: (l, 0, j)),
            pl.BlockSpec((None, 1, D_MODEL), lambda l, j: (l, 0, j)),
        ],
        out_specs=pl.BlockSpec((None, r, D_MODEL), lambda l, j: (l, 0, j)),
        out_shape=jax.ShapeDtypeStruct((DEPTH, r, 3 * D_MODEL), F32),
        compiler_params=_params("arbitrary", "arbitrary"),
        name="adaln_modulation",
    )(cond, w_ada, b_ada.reshape(DEPTH, 1, 3 * D_MODEL))


def _inproj_kernel(x_ref, g_ref, scale_ref, shift_ref, w_ref, gmg_ref, gmw_ref, gmb_ref, ones_ref,
                   qg_ref, kg_ref, *rest, tm, latent):
    if latent:
        cos_ref, sin_ref = rest[:2]
        rest = rest[2:]
    ya_ref, bq_ref, bk_ref, bv_ref, cx_ref, dq_ref, dk_ref, dv_ref = rest

    h = _modulated_norm(x_ref[...], g_ref[...], scale_ref[...], shift_ref[...])
    p = _dot(h.astype(BF16), w_ref[...])

    u = p[:, 0:256]
    v = p[:, 256:512]
    vn = (v * lax.rsqrt(jnp.mean(v * v, axis=-1, keepdims=True) + EPS)) * gmg_ref[...]
    vnb = vn.astype(BF16)
    lane_grp = lax.broadcasted_iota(jnp.int32, (CHUNK, BRANCH_W), 1) // HEAD_DIM
    for c in range(tm // CHUNK):
        r = _dot(gmw_ref[...], vnb[c * CHUNK:(c + 1) * CHUNK, :])
        sp = r[0:CHUNK]
        for g in range(1, GM_GROUPS):
            sp = jnp.where(lane_grp == g, r[g * CHUNK:(g + 1) * CHUNK], sp)
        sp = sp + gmb_ref[...]
        ya_ref[c * CHUNK:(c + 1) * CHUNK, :] = u[c * CHUNK:(c + 1) * CHUNK, :] * sp

    bq_ref[...] = (p[:, 512:768] * QK_SCALE_LOG2).astype(bq_ref.dtype)
    bk_ref[...] = p[:, 768:1024].astype(bk_ref.dtype)
    bv_ref[...] = _vt_tile(p[:, 1024:1280], NA_HEADS) if latent else p[:, 1024:1280]
    cx_ref[...] = p[:, 1280:1536]
    ones_blk = ones_ref[...]
    dq = _head_rmsnorm(p[:, 1536:1792], ones_blk, qg_ref[...])
    dk = _head_rmsnorm(p[:, 1792:1920], ones_blk[0:128, 0:128], kg_ref[...])
    if latent:
        cos = cos_ref[...]
        sin = sin_ref[...]
        dq = _rope(dq, cos, sin)
        dk = _rope(dk, cos[:, 0:128], sin[:, 0:128])
    dq_ref[...] = (dq * QK_SCALE_LOG2).astype(dq_ref.dtype)
    dk_ref[...] = dk.astype(dk_ref.dtype)
    dv_ref[...] = _vt_tile(p[:, 1920:2048], GQA_KV) if latent else p[:, 1920:2048]


def _inproj(x, mod_scale, mod_shift, lw, consts, *, batch, seq, tm, latent):
    n = batch * seq
    tiles_per_seq = seq // tm
    halves = batch // SUBLANES
    per_batch_mod = mod_scale.shape[0] > 1

    def mod_idx(i):
        return ((i // tiles_per_seq) if per_batch_mod else 0, 0, 0)

    row = lambda i: (i, 0)
    kv_dtype = BF16 if latent else F32
    in_specs = [
        pl.BlockSpec((tm, D_MODEL), row),
        _full((1, D_MODEL)),
        pl.BlockSpec((None, 1, D_MODEL), mod_idx),
        pl.BlockSpec((None, 1, D_MODEL), mod_idx),
        _full((D_MODEL, 2048)),
        _full((1, BRANCH_W)),
        _full((GM_GROUPS * CHUNK, CHUNK)),
        _full((CHUNK, BRANCH_W)),
        _full((BRANCH_W, BRANCH_W)),
        _full((1, BRANCH_W)),
        _full((1, 128)),
    ]
    args = [x, lw["norm_g"], mod_scale, mod_shift, lw["w_proj"], lw["gm_v_g"], lw["gm_w"], lw["gm_bias"],
            consts["ones_blk"], lw["gqa_q_g"], lw["gqa_k_g"]]
    if latent:
        rope_idx = lambda i: (i % tiles_per_seq, 0)
        in_specs += [pl.BlockSpec((tm, BRANCH_W), rope_idx), pl.BlockSpec((tm, BRANCH_W), rope_idx)]
        args += [consts["rope_cos"], consts["rope_sin"]]

    def cx_idx(i):
        b = i // tiles_per_seq
        return (b // SUBLANES, i % tiles_per_seq, b % SUBLANES)

    def vt_spec(heads):
        return pl.BlockSpec((None, heads * VT_ROWS, tm), lambda i: (i // tiles_per_seq, 0, i % tiles_per_seq))

    def vt_shape(heads):
        return jax.ShapeDtypeStruct((batch, heads * VT_ROWS, seq), BF16)

    out_specs = [
        pl.BlockSpec((tm, BRANCH_W), row),
        pl.BlockSpec((tm, BRANCH_W), row),
        pl.BlockSpec((tm, BRANCH_W), row),
        vt_spec(NA_HEADS) if latent else pl.BlockSpec((tm, BRANCH_W), row),
        pl.BlockSpec((None, tm, BRANCH_W), cx_idx),
        pl.BlockSpec((tm, BRANCH_W), row),
        pl.BlockSpec((tm, 128), row),
        vt_spec(GQA_KV) if latent else pl.BlockSpec((tm, 128), row),
    ]
    out_shape = [
        jax.ShapeDtypeStruct((n, BRANCH_W), F32),
        jax.ShapeDtypeStruct((n, BRANCH_W), BF16),
        jax.ShapeDtypeStruct((n, BRANCH_W), kv_dtype),
        vt_shape(NA_HEADS) if latent else jax.ShapeDtypeStruct((n, BRANCH_W), F32),
        jax.ShapeDtypeStruct((halves, seq, SUBLANES * BRANCH_W), F32),
        jax.ShapeDtypeStruct((n, BRANCH_W), BF16),
        jax.ShapeDtypeStruct((n, 128), kv_dtype),
        vt_shape(GQA_KV) if latent else jax.ShapeDtypeStruct((n, 128), F32),
    ]
    return pl.pallas_call(
        functools.partial(_inproj_kernel, tm=tm, latent=latent),
        grid=(n // tm,),
        in_specs=in_specs,
        out_specs=out_specs,
        out_shape=out_shape,
        compiler_params=_params("parallel"),
        name="inproj_latent" if latent else "inproj_ctx",
    )(*args)


def _block_attn_kernel(q_ref, k_ref, vt_ref, *rest, tq, kvh, rep, lk, ck, cache_len):
    if cache_len:
        ck_ref, cvt_ref, o_ref, s_scr = rest
    else:
        o_ref, s_scr = rest
    steps = []
    for j in range(kvh):
        steps += [(j, c, False) for c in range(lk // ck)]
        if cache_len:
            steps.append((j, 0, True))
    per_head = len(steps) // kvh
    qs = [jnp.concatenate([q_ref[:, (j * rep + g) * HEAD_DIM:(j * rep + g + 1) * HEAD_DIM] for g in range(rep)],
                          axis=0) for j in range(kvh)]

    def logits(t):
        j, c, cached = steps[t]
        lanes = slice(j * HEAD_DIM, (j + 1) * HEAD_DIM)
        kc = ck_ref[:, lanes] if cached else k_ref[c * ck:(c + 1) * ck, lanes]
        st = _dot_nt(kc.astype(BF16), qs[j])
        s_scr[t % 2, 0:st.shape[0], :] = st
        return jnp.max(st, axis=0, keepdims=True)

    cmax = logits(0)
    m = acc = None
    heads_t = []
    for t, (j, c, cached) in enumerate(steps):
        nxt = logits(t + 1) if t + 1 < len(steps) else None
        n_keys = cache_len if cached else ck
        vrows = slice(j * VT_ROWS, (j + 1) * VT_ROWS)
        vt = cvt_ref[vrows, :] if cached else vt_ref[vrows, c * ck:(c + 1) * ck]
        st = s_scr[t % 2, 0:n_keys, :]
        if t % per_head == 0:
            m = cmax
            acc = _dot(vt, jnp.exp2(st - m).astype(BF16))
        else:
            m_new = jnp.maximum(m, cmax)
            acc = jnp.exp2(m - m_new) * acc + _dot(vt, jnp.exp2(st - m_new).astype(BF16))
            m = m_new
        if t % per_head == per_head - 1:
            o = acc[0:HEAD_DIM] / acc[HEAD_DIM:HEAD_DIM + 1]
            heads_t += [o[:, g * tq:(g + 1) * tq] for g in range(rep)]
        cmax = nxt
    o_ref[...] = jnp.concatenate(heads_t, axis=0).T


def _vt_layout(v, batch, seq, kvh):
    vt = v.reshape(batch, seq, kvh, HEAD_DIM).transpose(0, 2, 3, 1).astype(BF16)
    ones = jnp.ones((batch, kvh, VT_ROWS - HEAD_DIM, seq), BF16)
    return jnp.concatenate([vt, ones], axis=2).reshape(batch, kvh * VT_ROWS, seq)


def _block_attention(q, k, vt, cache_k, cache_vt, *, batch, seq, kvh, rep, tq, ck, name):
    n = batch * seq
    wkv = kvh * HEAD_DIM
    nq = seq // tq
    cache_len = 0 if cache_k is None else cache_k.shape[1]
    in_specs = [
        pl.BlockSpec((tq, BRANCH_W), lambda b, i: (b * nq + i, 0)),
        pl.BlockSpec((None, seq, wkv), lambda b, i: (b, 0, 0)),
        pl.BlockSpec((None, kvh * VT_ROWS, seq), lambda b, i: (b, 0, 0)),
    ]
    args = [q, k.reshape(batch, seq, wkv), vt]
    if cache_len:
        in_specs += [pl.BlockSpec((None, cache_len, wkv), lambda b, i: (b, 0, 0)),
                     pl.BlockSpec((None, kvh * VT_ROWS, cache_len), lambda b, i: (b, 0, 0))]
        args += [cache_k, cache_vt]
    return pl.pallas_call(
        functools.partial(_block_attn_kernel, tq=tq, kvh=kvh, rep=rep, lk=seq, ck=ck, cache_len=cache_len),
        grid=(batch, nq),
        in_specs=in_specs,
        out_specs=pl.BlockSpec((tq, BRANCH_W), lambda b, i: (b * nq + i, 0)),
        out_shape=jax.ShapeDtypeStruct((n, BRANCH_W), F32),
        scratch_shapes=[pltpu.VMEM((2, max(ck, cache_len), rep * tq), F32)],
        compiler_params=_params("parallel", "arbitrary"),
        name=name,
    )(*args)


NA_QROWS = 4
NA_KROWS = 12
NA_TQ = NA_QROWS * GRID_W
NA_TK = NA_KROWS * GRID_W


def _na_kernel(q_ref, k_ref, vt_ref, ck_ref, cvt_ref, bias_ref, o_ref, s_scr, *, rows, cache_len):
    i = pl.program_id(1)
    kbase = jnp.clip(i * NA_QROWS - NA_KH // 2, 0, rows - NA_KROWS)
    r0 = pl.multiple_of(kbase * GRID_W, 2 * GRID_W)

    def logits(h):
        lanes = slice(h * HEAD_DIM, (h + 1) * HEAD_DIM)
        qh = q_ref[:, lanes]
        s_win = _dot_nt(k_ref[pl.ds(r0, NA_TK), lanes], qh) + bias_ref[h]
        s_ctx = _dot_nt(ck_ref[:, lanes], qh)
        s_scr[h % 2, 0:NA_TK, :] = s_win
        s_scr[h % 2, NA_TK:NA_TK + cache_len, :] = s_ctx
        return jnp.maximum(jnp.max(s_win, axis=0, keepdims=True), jnp.max(s_ctx, axis=0, keepdims=True))

    m = logits(0)
    heads_t = []
    for h in range(NA_HEADS):
        nxt = logits(h + 1) if h + 1 < NA_HEADS else None
        vrows = slice(h * VT_ROWS, (h + 1) * VT_ROWS)
        p_win = jnp.exp2(s_scr[h % 2, 0:NA_TK, :] - m).astype(BF16)
        p_ctx = jnp.exp2(s_scr[h % 2, NA_TK:NA_TK + cache_len, :] - m).astype(BF16)
        acc = _dot(vt_ref[vrows, pl.ds(r0, NA_TK)], p_win) + _dot(cvt_ref[vrows, :], p_ctx)
        heads_t.append(acc[0:HEAD_DIM] / acc[HEAD_DIM:HEAD_DIM + 1])
        m = nxt
    o_ref[...] = jnp.concatenate(heads_t, axis=0).T


def _na_attention(q, k, vt, cache_k, cache_vt, bias, *, batch, seq):
    n = batch * seq
    rows = seq // GRID_W
    nq = rows // NA_QROWS
    cache_len = cache_k.shape[1]
    vt_rows = NA_HEADS * VT_ROWS

    def bias_idx(b, i):
        return (jnp.where(i == 0, 0, jnp.where(i == nq - 1, 2, 1)), 0, 0, 0)

    return pl.pallas_call(
        functools.partial(_na_kernel, rows=rows, cache_len=cache_len),
        grid=(batch, nq),
        in_specs=[
            pl.BlockSpec((NA_TQ, BRANCH_W), lambda b, i: (b * nq + i, 0)),
            pl.BlockSpec((None, seq, BRANCH_W), lambda b, i: (b, 0, 0)),
            pl.BlockSpec((None, vt_rows, seq), lambda b, i: (b, 0, 0)),
            pl.BlockSpec((None, cache_len, BRANCH_W), lambda b, i: (b, 0, 0)),
            pl.BlockSpec((None, vt_rows, cache_len), lambda b, i: (b, 0, 0)),
            pl.BlockSpec((None, NA_HEADS, NA_TK, NA_TQ), bias_idx),
        ],
        out_specs=pl.BlockSpec((NA_TQ, BRANCH_W), lambda b, i: (b * nq + i, 0)),
        out_shape=jax.ShapeDtypeStruct((n, BRANCH_W), F32),
        scratch_shapes=[pltpu.VMEM((2, NA_TK + cache_len, NA_TQ), F32)],
        compiler_params=_params("parallel", "arbitrary"),
        name="na_latent",
    )(q, k.reshape(batch, seq, BRANCH_W), vt, cache_k, cache_vt, bias)


def _na_bias_tables(rel_bias, rows):
    nq = rows // NA_QROWS
    kh = min(NA_KH, rows)
    n_dr, n_dc = 2 * NA_KH - 1, 2 * NA_KW - 1
    cq = np.arange(GRID_W)
    start_c = np.clip(cq - NA_KW // 2, 0, GRID_W - NA_KW)
    col_ok = (cq[None, :] >= start_c[:, None]) & (cq[None, :] < start_c[:, None] + NA_KW)
    dc = np.clip(cq[None, :] - cq[:, None], -(NA_KW - 1), NA_KW - 1) + (NA_KW - 1)
    onehot = (dc.T.reshape(1, -1) == np.arange(n_dc)[:, None]).astype(np.float32)
    per_dr = jnp.einsum("hrc,cx->hrx", rel_bias.astype(F32), jnp.asarray(onehot), precision=lax.Precision.HIGHEST)
    per_dr = jnp.where(col_ok.T.reshape(1, 1, -1), per_dr * math.log2(math.e), NEG_INF)
    masked = jnp.full((NA_HEADS, 1, GRID_W * GRID_W), NEG_INF, F32)
    per_dr = jnp.concatenate([per_dr, masked], axis=1).reshape(NA_HEADS, n_dr + 1, GRID_W, GRID_W)
    tabs = []
    for blk in (0, 1, nq - 1):
        kbase = int(np.clip(blk * NA_QROWS - NA_KH // 2, 0, rows - NA_KROWS))
        r = blk * NA_QROWS + np.arange(NA_QROWS)
        start_r = np.clip(r - kh // 2, 0, rows - kh)
        kr = kbase + np.arange(NA_KROWS)
        row_ok = (kr[None, :] >= start_r[:, None]) & (kr[None, :] < start_r[:, None] + kh)
        dr = np.where(row_ok, kr[None, :] - r[:, None] + (NA_KH - 1), n_dr)
        tabs.append(jnp.concatenate(
            [jnp.concatenate([per_dr[:, int(dr[a, b])] for a in range(NA_QROWS)], axis=-1)
             for b in range(NA_KROWS)], axis=-2))
    return jnp.stack(tabs, axis=0)


def _gelu_tanh(y):
    return 0.5 * y * (1.0 + jnp.tanh(math.sqrt(2.0 / math.pi) * (y + 0.044715 * (y * y * y))))


def _s5_kernel(x_ref, bmat_ref, cmat_ref, abar_ref, s0_ref, *rest, steps, reverse, nchunk):
    if reverse:
        xprev_ref, yprev_ref, dvec_ref, wglu_ref, bglu_ref, y_ref, sfin_ref, bu_ref, sb_ref, st_ref = rest
    else:
        y_ref, sfin_ref, bu_ref, sb_ref, st_ref = rest
    j = pl.program_id(1)

    @pl.when(j == 0)
    def _():
        st_ref[...] = s0_ref[...]
        sb_ref[...] = jnp.zeros_like(sb_ref)

    y = _dot(sb_ref[...], cmat_ref[...])
    bu_ref[...] = _dot(x_ref[...].astype(BF16), bmat_ref[...])
    if reverse:
        y = yprev_ref[...] + y + dvec_ref[...] * xprev_ref[...]
        y = _gelu_tanh(y)
        y = y * jax.nn.sigmoid(_dot(y.astype(BF16), wglu_ref[...]) + bglu_ref[...])
    y_ref[...] = y

    @pl.when(j < nchunk)
    def _():
        a_re = jnp.broadcast_to(abar_ref[:, 0:SSM_N], (SUBLANES, SSM_N))
        a_im = jnp.broadcast_to(abar_ref[:, SSM_N:], (SUBLANES, SSM_N))
        pair_rows = 2 * SUBLANES

        def pair(i, carry):
            s_re, s_im = carry
            blk = (steps // 2 - 1 - i) if reverse else i
            r0 = pl.multiple_of(blk * pair_rows, pair_rows)
            out_re, out_im = [None, None], [None, None]
            for half in ((1, 0) if reverse else (0, 1)):
                rr = pl.multiple_of(r0 + half * SUBLANES, SUBLANES)
                n_re = a_re * s_re - a_im * s_im + bu_ref[pl.ds(rr, SUBLANES), 0:SSM_N]
                n_im = a_re * s_im + a_im * s_re + bu_ref[pl.ds(rr, SUBLANES), SSM_N:]
                out_re[half], out_im[half] = n_re, n_im
                s_re, s_im = n_re, n_im
            sb_ref[pl.ds(r0, pair_rows), 0:SSM_N] = jnp.concatenate(out_re, axis=0).astype(BF16)
            sb_ref[pl.ds(r0, pair_rows), SSM_N:] = jnp.concatenate(out_im, axis=0).astype(BF16)
            return s_re, s_im

        s_re, s_im = lax.fori_loop(0, steps // 2, pair, (st_ref[:, 0:SSM_N], st_ref[:, SSM_N:]), unroll=4)
        st_ref[:, 0:SSM_N] = s_re
        st_ref[:, SSM_N:] = s_im

    sfin_ref[...] = st_ref[...]


def _s5_pass(x3, sp, s0, yprev, lw, *, steps, reverse):
    halves, rows, _ = x3.shape
    nchunk = rows // (steps * SUBLANES)
    tr = steps * SUBLANES

    def chunk_of(step):
        step = jnp.clip(step, 0, nchunk - 1)
        return (nchunk - 1 - step) if reverse else step

    cur = lambda h, j: (h, chunk_of(j), 0)
    prev = lambda h, j: (h, chunk_of(j - 1), 0)

    in_specs = [
        pl.BlockSpec((None, tr, BRANCH_W), cur),
        _full((BRANCH_W, 2 * SSM_N)),
        _full((2 * SSM_N, BRANCH_W)),
        _full((1, 2 * SSM_N)),
        pl.BlockSpec((None, SUBLANES, 2 * SSM_N), lambda h, j: (h, 0, 0)),
    ]
    args = [x3, sp["bmat"], sp["cmat"], sp["abar"], s0]
    if reverse:
        in_specs += [pl.BlockSpec((None, tr, BRANCH_W), prev), pl.BlockSpec((None, tr, BRANCH_W), prev),
                     _full((1, BRANCH_W)), _full((BRANCH_W, BRANCH_W)), _full((1, BRANCH_W))]
        args += [x3, yprev, lw["ssm_d"], lw["w_glu"], lw["b_glu"]]
    return pl.pallas_call(
        functools.partial(_s5_kernel, steps=steps, reverse=reverse, nchunk=nchunk),
        grid=(halves, nchunk + 1),
        in_specs=in_specs,
        out_specs=[pl.BlockSpec((None, tr, BRANCH_W), prev),
                   pl.BlockSpec((None, SUBLANES, 2 * SSM_N), lambda h, j: (h, 0, 0))],
        out_shape=[jax.ShapeDtypeStruct(x3.shape, F32),
                   jax.ShapeDtypeStruct((halves, SUBLANES, 2 * SSM_N), F32)],
        scratch_shapes=[pltpu.VMEM((tr, 2 * SSM_N), F32), pltpu.VMEM((tr, 2 * SSM_N), BF16),
                        pltpu.VMEM((SUBLANES, 2 * SSM_N), F32)],
        compiler_params=_params("parallel", "arbitrary"),
        name="s5_reverse" if reverse else "s5_forward",
    )(*args)


def _s5_discretise(a_re, a_im, log_dt, b_re, b_im, c_re, c_im):
    dt = jnp.exp(log_dt)[:, None]
    mag = jnp.exp(dt * a_re)
    ab_re, ab_im = mag * jnp.cos(dt * a_im), mag * jnp.sin(dt * a_im)
    den = a_re * a_re + a_im * a_im
    nr, ni = ab_re - 1.0, ab_im
    coef_re = ((nr * a_re + ni * a_im) / den)[..., None]
    coef_im = ((ni * a_re - nr * a_im) / den)[..., None]
    bb_re = coef_re * b_re - coef_im * b_im
    bb_im = coef_re * b_im + coef_im * b_re
    eye = jnp.eye(SSM_G, dtype=F32)

    def in_blocks(bb):
        return jnp.einsum("gpi,gk->gikp", bb, eye).reshape(SSM_G * SSM_H, SSM_N)

    def out_blocks(cc):
        return jnp.einsum("gip,gk->gpki", cc, eye).reshape(SSM_N, SSM_G * SSM_H)

    bmat = jnp.concatenate([in_blocks(bb_re), in_blocks(bb_im)], axis=1).astype(BF16)
    cmat = jnp.concatenate([out_blocks(c_re), -out_blocks(c_im)], axis=0).astype(BF16)
    abar = jnp.concatenate([ab_re.reshape(1, SSM_N), ab_im.reshape(1, SSM_N)], axis=1)
    return dict(bmat=bmat, cmat=cmat, abar=abar)


def _state_to_rows(s):
    b = s.shape[0]
    flat = jnp.concatenate([s[..., 0].reshape(b, SSM_N), s[..., 1].reshape(b, SSM_N)], axis=1)
    return flat.reshape(b // SUBLANES, SUBLANES, 2 * SSM_N)


def _rows_to_state(r):
    b = r.shape[0] * SUBLANES
    flat = r.reshape(b, 2, SSM_G, SSM_P)
    return jnp.stack([flat[:, 0], flat[:, 1]], axis=-1)


def _merge_kernel(x_ref, g_ref, scale_ref, shift_ref, gate_ref, ya_ref, yb_ref, yc_ref, yd_ref,
                  wz_ref, wg_ref, wbr_ref, wout_ref, fg_ref, o_ref, *, last):
    x = x_ref[...]
    hb = _modulated_norm(x, g_ref[...], scale_ref[...], shift_ref[...]).astype(BF16)
    z = _dot(hb, wz_ref[...])
    merged = None
    for n, y_ref in enumerate((ya_ref, yb_ref, yc_ref, yd_ref)):
        zn = z[:, n * BRANCH_W:(n + 1) * BRANCH_W]
        yn = y_ref[...] * (zn * jax.nn.sigmoid(zn))
        t = _dot(yn.astype(BF16), wbr_ref[n * BRANCH_W:(n + 1) * BRANCH_W, :])
        gn = _dot(hb, wg_ref[:, n * D_MODEL:(n + 1) * D_MODEL])
        term = jax.nn.sigmoid(gn) * t
        merged = term if merged is None else merged + term
    out = x + gate_ref[...] * _dot(merged.astype(BF16), wout_ref[...])
    if last:
        out = (out * lax.rsqrt(jnp.mean(out * out, axis=-1, keepdims=True) + EPS)) * fg_ref[...]
    o_ref[...] = out


def _merge(x, mod_scale, mod_shift, mod_gate, ya, yb, yc3, yd, lw, final_g, *, batch, seq, tm, last, name):
    n = batch * seq
    tiles_per_seq = seq // tm
    per_batch_mod = mod_scale.shape[0] > 1

    def mod_idx(i):
        return ((i // tiles_per_seq) if per_batch_mod else 0, 0, 0)

    def yc_idx(i):
        b = i // tiles_per_seq
        return (b // SUBLANES, i % tiles_per_seq, b % SUBLANES)

    row = lambda i: (i, 0)
    ytile = pl.BlockSpec((tm, BRANCH_W), row)
    return pl.pallas_call(
        functools.partial(_merge_kernel, last=last),
        grid=(n // tm,),
        in_specs=[
            pl.BlockSpec((tm, D_MODEL), row),
            _full((1, D_MODEL)),
            pl.BlockSpec((None, 1, D_MODEL), mod_idx),
            pl.BlockSpec((None, 1, D_MODEL), mod_idx),
            pl.BlockSpec((None, 1, D_MODEL), mod_idx),
            ytile, ytile,
            pl.BlockSpec((None, tm, BRANCH_W), yc_idx),
            ytile,
            _full((D_MODEL, 4 * BRANCH_W)),
            _full((D_MODEL, 4 * D_MODEL)),
            _full((4 * BRANCH_W, D_MODEL)),
            _full((D_MODEL, D_MODEL)),
            _full((1, D_MODEL)),
        ],
        out_specs=pl.BlockSpec((tm, D_MODEL), row),
        out_shape=jax.ShapeDtypeStruct((n, D_MODEL), F32),
        compiler_params=_params("parallel"),
        name=name,
    )(x, lw["norm_g"], mod_scale, mod_shift, mod_gate, ya, yb, yc3, yd,
      lw["w_z"], lw["w_g"], lw["w_branch"], lw["w_out"], final_g)


def _rope_tables(seq):
    t = np.arange(seq)
    row = (t // GRID_W).astype(np.float32)
    col = (t % GRID_W).astype(np.float32)
    nf = HEAD_DIM // 4
    freqs = jnp.asarray(ROPE_BASE, F32) ** (-jnp.arange(nf, dtype=F32) / nf)
    ang_r = jnp.asarray(row)[:, None] * freqs[None, :]
    ang_c = jnp.asarray(col)[:, None] * freqs[None, :]
    cos = jnp.concatenate([jnp.cos(ang_r)] * 2 + [jnp.cos(ang_c)] * 2, axis=1)
    sin = jnp.concatenate([-jnp.sin(ang_r), jnp.sin(ang_r), -jnp.sin(ang_c), jnp.sin(ang_c)], axis=1)
    return jnp.tile(cos, (1, GQA_HEADS)), jnp.tile(sin, (1, GQA_HEADS))


def _layer_weights(l, norm_g, w_in, gm_v_g, gm_ws, gm_b, ssm_d, w_glu, b_glu, gqa_q_g, gqa_k_g, w_branch, w_out):
    w = w_in[l]
    w_proj = jnp.concatenate([w[:, 0:512], w[:, 768:1536], w[:, 1792:2048], w[:, 2304:2816]], axis=1).astype(BF16)
    w_z = jnp.concatenate([w[:, 512:768], w[:, 1536:1792], w[:, 2048:2304], w[:, 2816:3072]], axis=1).astype(BF16)
    return dict(
        norm_g=norm_g[l].reshape(1, D_MODEL),
        w_proj=w_proj,
        w_z=w_z,
        w_g=w[:, 3072:].astype(BF16),
        gm_v_g=gm_v_g[l].reshape(1, BRANCH_W),
        gm_w=gm_ws[l].reshape(GM_GROUPS * CHUNK, CHUNK).astype(BF16),
        gm_bias=jnp.repeat(gm_b[l].T, HEAD_DIM, axis=1),
        ssm_d=ssm_d[l].reshape(1, BRANCH_W),
        w_glu=w_glu[l].astype(BF16),
        b_glu=b_glu[l].reshape(1, BRANCH_W),
        gqa_q_g=jnp.tile(gqa_q_g[l], GQA_HEADS).reshape(1, BRANCH_W),
        gqa_k_g=jnp.tile(gqa_k_g[l], GQA_KV).reshape(1, 128),
        w_branch=w_branch[l].reshape(4 * BRANCH_W, D_MODEL).astype(BF16),
        w_out=w_out[l].astype(BF16),
    )


def _trunk_layer(x, mods, lw, sps, consts, final_g, cache, *, batch, seq, latent, last):
    scale, shift, gate = mods
    tm = 512 if latent else 256
    ya, bq, bk, bv, cx, dq, dk, dv = _inproj(x, scale, shift, lw, consts, batch=batch, seq=seq, tm=tm, latent=latent)
    halves = batch // SUBLANES
    cx3 = cx.reshape(halves, seq * SUBLANES, BRANCH_W)
    if latent:
        yb = _na_attention(bq, bk, bv, cache["na_k"], cache["na_vt"], cache["na_bias"], batch=batch, seq=seq)
        yd = _block_attention(dq, dk, dv, cache["gqa_k"], cache["gqa_vt"], batch=batch, seq=seq,
                              kvh=GQA_KV, rep=GQA_HEADS // GQA_KV, tq=256, ck=512, name="gqa_latent")
        s0 = cache["ssm"]
    else:
        yb = _block_attention(bq, bk, _vt_layout(bv, batch, seq, NA_HEADS), None, None, batch=batch, seq=seq,
                              kvh=NA_HEADS, rep=1, tq=seq, ck=seq, name="na_ctx")
        yd = _block_attention(dq, dk, _vt_layout(dv, batch, seq, GQA_KV), None, None, batch=batch, seq=seq,
                              kvh=GQA_KV, rep=GQA_HEADS // GQA_KV, tq=seq, ck=seq, name="gqa_ctx")
        zero = jnp.zeros((halves, SUBLANES, 2 * SSM_N), F32)
        s0 = (zero, zero)
    steps = 64
    yf, sf = _s5_pass(cx3, sps[0], s0[0], None, lw, steps=steps, reverse=False)
    yc3, sr = _s5_pass(cx3, sps[1], s0[1], yf, lw, steps=steps, reverse=True)
    yc3 = yc3.reshape(halves, seq, SUBLANES * BRANCH_W)
    x_new = _merge(x, scale, shift, gate, ya, yb, yc3, yd, lw, final_g, batch=batch, seq=seq, tm=tm, last=last,
                   name="merge_latent" if latent else "merge_ctx")
    return x_new, (bk, bv, dk, dv, sf, sr)


def kernel(x_prompt, x_sample, c, cache_na_k, cache_na_v, cache_gqa_k, cache_gqa_v, state_ssm, c_ctx,
           norm_g, w_ada, b_ada, w_in, gm_v_g, gm_ws, gm_b, na_rel_bias, ssm_a_re, ssm_a_im, ssm_log_dt,
           ssm_b_re, ssm_b_im, ssm_c_re, ssm_c_im, ssm_d, w_glu, b_glu, gqa_q_g, gqa_k_g, w_branch, w_out,
           final_g):
    bc, lc, _ = x_prompt.shape
    bl, ll, _ = x_sample.shape
    past = cache_na_k.shape[2]

    n_rows = 16
    cond = jnp.zeros((n_rows, D_MODEL), F32).at[0].set(c_ctx).at[1:1 + bl].set(c)
    mod = _modulation(cond, w_ada, b_ada)

    def mods_of(l, lo, hi):
        m = mod[l, lo:hi].reshape(hi - lo, 1, 3 * D_MODEL)
        return m[..., 0:D_MODEL], m[..., D_MODEL:2 * D_MODEL], m[..., 2 * D_MODEL:]

    cos, sin = _rope_tables(ll)
    ones_blk = jnp.asarray(np.kron(np.eye(BRANCH_W // HEAD_DIM), np.ones((HEAD_DIM, HEAD_DIM))), BF16)
    consts = dict(rope_cos=cos, rope_sin=sin, ones_blk=ones_blk)
    fg = final_g.reshape(1, D_MODEL)

    xp = x_prompt.reshape(bc * lc, D_MODEL)
    xs = x_sample.reshape(bl * ll, D_MODEL)
    na_k_l, na_v_l, gqa_k_l, gqa_v_l, ssm_l = [], [], [], [], []
    for l in range(DEPTH):
        lw = _layer_weights(l, norm_g, w_in, gm_v_g, gm_ws, gm_b, ssm_d, w_glu, b_glu, gqa_q_g, gqa_k_g,
                            w_branch, w_out)
        sps = [_s5_discretise(ssm_a_re[l, d], ssm_a_im[l, d], ssm_log_dt[l, d], ssm_b_re[l, d], ssm_b_im[l, d],
                              ssm_c_re[l, d], ssm_c_im[l, d]) for d in range(2)]
        last = l == DEPTH - 1
        shift, scale, gate = mods_of(l, 0, 1)
        xp, (k_na, v_na, k_g, v_g, sf, sr) = _trunk_layer(
            xp, (scale, shift, gate), lw, sps, consts, fg, None, batch=bc, seq=lc, latent=False, last=last)
        na_k_l.append(k_na.reshape(bc, lc, NA_HEADS, HEAD_DIM))
        na_v_l.append(v_na.reshape(bc, lc, NA_HEADS, HEAD_DIM))
        gqa_k_l.append(k_g.reshape(bc, lc, GQA_KV, HEAD_DIM))
        gqa_v_l.append(v_g.reshape(bc, lc, GQA_KV, HEAD_DIM))
        ssm_l.append(jnp.stack([_rows_to_state(sf), _rows_to_state(sr)], axis=1))
        shift, scale, gate = mods_of(l, 1, 1 + bl)
        cache = dict(
            na_k=cache_na_k[:, l].reshape(bl, past, BRANCH_W).astype(BF16),
            na_vt=_vt_layout(cache_na_v[:, l], bl, past, NA_HEADS),
            gqa_k=cache_gqa_k[:, l].reshape(bl, past, GQA_KV * HEAD_DIM).astype(BF16),
            gqa_vt=_vt_layout(cache_gqa_v[:, l], bl, past, GQA_KV),
            ssm=(_state_to_rows(state_ssm[:, l, 0]), _state_to_rows(state_ssm[:, l, 1])),
            na_bias=_na_bias_tables(na_rel_bias[l], ll // GRID_W),
        )
        xs, _ = _trunk_layer(xs, (scale, shift, gate), lw, sps, consts, fg, cache,
                             batch=bl, seq=ll, latent=True, last=last)
    return (xp.reshape(bc, lc, D_MODEL), xs.reshape(bl, ll, D_MODEL),
            jnp.stack(na_k_l, axis=1), jnp.stack(na_v_l, axis=1),
            jnp.stack(gqa_k_l, axis=1), jnp.stack(gqa_v_l, axis=1), jnp.stack(ssm_l, axis=1))
```

```python
import functools
import math

import numpy as np
import jax
import jax.numpy as jnp
from jax import lax
from jax.experimental import pallas as pl
from jax.experimental.pallas import tpu as pltpu

D_MODEL = 1024
DEPTH = 2
GRID_W = 64
BRANCH_W = 256
HEAD_DIM = 64
CHUNK = 128
GM_GROUPS = 4
NA_HEADS = 4
NA_KH = 8
NA_KW = 16
SSM_H = 16
SSM_G = 16
SSM_P = 64
SSM_N = SSM_G * SSM_P
GQA_HEADS = 4
GQA_KV = 2
ROPE_BASE = 10000.0
EPS = 1e-6
NEG_INF = -1e30
ATTN_SCALE = HEAD_DIM ** -0.5

SUBLANES = 8
VMEM_LIMIT = 56 * 1024 * 1024

F32 = jnp.float32
BF16 = jnp.bfloat16


def _params(*sem):
    return pltpu.CompilerParams(dimension_semantics=sem, vmem_limit_bytes=VMEM_LIMIT)


def _dot(a, b):
    return jnp.dot(a, b, preferred_element_type=F32)


def _dot_nt(a, b):
    return lax.dot_general(a, b, (((1,), (1,)), ((), ())), preferred_element_type=F32)


def _full(shape):
    nd = len(shape)
    return pl.BlockSpec(shape, lambda *_: (0,) * nd)


def _modulated_norm(x, g, scale, shift):
    y = x * lax.rsqrt(jnp.mean(x * x, axis=-1, keepdims=True) + EPS)
    return (y * g) * (1.0 + scale) + shift


def _group_sumsq(x, ones_blk):
    x2 = x * x
    hi = x2.astype(BF16)
    lo = (x2 - hi.astype(F32)).astype(BF16)
    return _dot(hi, ones_blk) + _dot(lo, ones_blk)


def _head_rmsnorm(x, ones_blk, g):
    ss = _group_sumsq(x, ones_blk)
    return (x * lax.rsqrt(ss * (1.0 / HEAD_DIM) + EPS)) * g


def _rope(x, cos, sin_signed):
    w = x.shape[1]
    lane = lax.broadcasted_iota(jnp.int32, x.shape, 1)
    up = pltpu.roll(x, 16, axis=1)
    dn = pltpu.roll(x, w - 16, axis=1)
    partner = jnp.where((lane & 16) != 0, up, dn)
    return x * cos + partner * sin_signed


VT_ROWS = HEAD_DIM + 16
QK_SCALE_LOG2 = ATTN_SCALE * math.log2(math.e)


def _vt_tile(v, heads):
    vt = v.T.astype(BF16)
    ones = jnp.ones((VT_ROWS - HEAD_DIM, v.shape[0]), BF16)
    parts = []
    for h in range(heads):
        parts += [vt[h * HEAD_DIM:(h + 1) * HEAD_DIM], ones]
    return jnp.concatenate(parts, axis=0)


def _mod_kernel(cond_ref, w_ref, b_ref, o_ref):
    c = cond_ref[...]
    s = (c * jax.nn.sigmoid(c)).astype(BF16)
    o_ref[...] = _dot(s, w_ref[...].astype(BF16)) + b_ref[...]


def _modulation(cond, w_ada, b_ada):
    r = cond.shape[0]
    nj = 3
    return pl.pallas_call(
        _mod_kernel,
        grid=(DEPTH, nj),
        in_specs=[
            pl.BlockSpec((r, D_MODEL), lambda l, j: (0, 0)),
            pl.BlockSpec((None, D_MODEL, D_MODEL), lambda l, j: (l, 0, j)),
            pl.BlockSpec((None, 1, D_MODEL), lambda l, j: (l, 0, j)),
        ],
        out_specs=pl.BlockSpec((None, r, D_MODEL), lambda l, j: (l, 0, j)),
        out_shape=jax.ShapeDtypeStruct((DEPTH, r, 3 * D_MODEL), F32),
        compiler_params=_params("arbitrary", "arbitrary"),
        name="adaln_modulation",
    )(cond, w_ada, b_ada.reshape(DEPTH, 1, 3 * D_MODEL))


INPROJ_SUB_ROWS = 256


def _inproj_kernel(x_ref, g_ref, scale_ref, shift_ref, w_ref, gmg_ref, gmw_ref, gmb_ref, ones_ref,
                   qg_ref, kg_ref, *rest, tm, latent):
    if latent:
        cos_ref, sin_ref = rest[:2]
        rest = rest[2:]
    ya_ref, bq_ref, bk_ref, bv_ref, cx_ref, dq_ref, dk_ref, dv_ref = rest

    lane_grp = lax.broadcasted_iota(jnp.int32, (CHUNK, BRANCH_W), 1) // HEAD_DIM
    ones_blk = ones_ref[...]
    sub = min(tm, INPROJ_SUB_ROWS)
    for r0 in range(0, tm, sub):
        rows = slice(r0, r0 + sub)
        h = _modulated_norm(x_ref[rows, :], g_ref[...], scale_ref[...], shift_ref[...])
        p = _dot(h.astype(BF16), w_ref[...])

        u = p[:, 0:256]
        v = p[:, 256:512]
        vn = (v * lax.rsqrt(jnp.mean(v * v, axis=-1, keepdims=True) + EPS)) * gmg_ref[...]
        vnb = vn.astype(BF16)
        for c in range(sub // CHUNK):
            r = _dot(gmw_ref[...], vnb[c * CHUNK:(c + 1) * CHUNK, :])
            sp = r[0:CHUNK]
            for g in range(1, GM_GROUPS):
                sp = jnp.where(lane_grp == g, r[g * CHUNK:(g + 1) * CHUNK], sp)
            sp = sp + gmb_ref[...]
            ya_ref[r0 + c * CHUNK:r0 + (c + 1) * CHUNK, :] = u[c * CHUNK:(c + 1) * CHUNK, :] * sp

        bq_ref[rows, :] = (p[:, 512:768] * QK_SCALE_LOG2).astype(bq_ref.dtype)
        bk_ref[rows, :] = p[:, 768:1024].astype(bk_ref.dtype)
        if latent:
            bv_ref[:, rows] = _vt_tile(p[:, 1024:1280], NA_HEADS)
        else:
            bv_ref[rows, :] = p[:, 1024:1280]
        cx_ref[rows, :] = p[:, 1280:1536]
        dq = _head_rmsnorm(p[:, 1536:1792], ones_blk, qg_ref[...])
        dk = _head_rmsnorm(p[:, 1792:1920], ones_blk[0:128, 0:128], kg_ref[...])
        if latent:
            cos = cos_ref[rows, :]
            sin = sin_ref[rows, :]
            dq = _rope(dq, cos, sin)
            dk = _rope(dk, cos[:, 0:128], sin[:, 0:128])
        dq_ref[rows, :] = (dq * QK_SCALE_LOG2).astype(dq_ref.dtype)
        dk_ref[rows, :] = dk.astype(dk_ref.dtype)
        if latent:
            dv_ref[:, rows] = _vt_tile(p[:, 1920:2048], GQA_KV)
        else:
            dv_ref[rows, :] = p[:, 1920:2048]


def _inproj(x, mod_scale, mod_shift, lw, consts, *, batch, seq, tm, latent):
    n = batch * seq
    tiles_per_seq = seq // tm
    halves = batch // SUBLANES
    per_batch_mod = mod_scale.shape[0] > 1

    def mod_idx(i):
        return ((i // tiles_per_seq) if per_batch_mod else 0, 0, 0)

    row = lambda i: (i, 0)
    kv_dtype = BF16 if latent else F32
    in_specs = [
        pl.BlockSpec((tm, D_MODEL), row),
        _full((1, D_MODEL)),
        pl.BlockSpec((None, 1, D_MODEL), mod_idx),
        pl.BlockSpec((None, 1, D_MODEL), mod_idx),
        _full((D_MODEL, 2048)),
        _full((1, BRANCH_W)),
        _full((GM_GROUPS * CHUNK, CHUNK)),
        _full((CHUNK, BRANCH_W)),
        _full((BRANCH_W, BRANCH_W)),
        _full((1, BRANCH_W)),
        _full((1, 128)),
    ]
    args = [x, lw["norm_g"], mod_scale, mod_shift, lw["w_proj"], lw["gm_v_g"], lw["gm_w"], lw["gm_bias"],
            consts["ones_blk"], lw["gqa_q_g"], lw["gqa_k_g"]]
    if latent:
        rope_idx = lambda i: (i % tiles_per_seq, 0)
        in_specs += [pl.BlockSpec((tm, BRANCH_W), rope_idx), pl.BlockSpec((tm, BRANCH_W), rope_idx)]
        args += [consts["rope_cos"], consts["rope_sin"]]

    def cx_idx(i):
        b = i // tiles_per_seq
        return (b // SUBLANES, i % tiles_per_seq, b % SUBLANES)

    def vt_spec(heads):
        return pl.BlockSpec((None, heads * VT_ROWS, tm), lambda i: (i // tiles_per_seq, 0, i % tiles_per_seq))

    def vt_shape(heads):
        return jax.ShapeDtypeStruct((batch, heads * VT_ROWS, seq), BF16)

    out_specs = [
        pl.BlockSpec((tm, BRANCH_W), row),
        pl.BlockSpec((tm, BRANCH_W), row),
        pl.BlockSpec((tm, BRANCH_W), row),
        vt_spec(NA_HEADS) if latent else pl.BlockSpec((tm, BRANCH_W), row),
        pl.BlockSpec((None, tm, BRANCH_W), cx_idx),
        pl.BlockSpec((tm, BRANCH_W), row),
        pl.BlockSpec((tm, 128), row),
        vt_spec(GQA_KV) if latent else pl.BlockSpec((tm, 128), row),
    ]
    out_shape = [
        jax.ShapeDtypeStruct((n, BRANCH_W), F32),
        jax.ShapeDtypeStruct((n, BRANCH_W), BF16),
        jax.ShapeDtypeStruct((n, BRANCH_W), kv_dtype),
        vt_shape(NA_HEADS) if latent else jax.ShapeDtypeStruct((n, BRANCH_W), F32),
        jax.ShapeDtypeStruct((halves, seq, SUBLANES * BRANCH_W), F32),
        jax.ShapeDtypeStruct((n, BRANCH_W), BF16),
        jax.ShapeDtypeStruct((n, 128), kv_dtype),
        vt_shape(GQA_KV) if latent else jax.ShapeDtypeStruct((n, 128), F32),
    ]
    return pl.pallas_call(
        functools.partial(_inproj_kernel, tm=tm, latent=latent),
        grid=(n // tm,),
        in_specs=in_specs,
        out_specs=out_specs,
        out_shape=out_shape,
        compiler_params=_params("parallel"),
        name="inproj_latent" if latent else "inproj_ctx",
    )(*args)


def _block_attn_kernel(q_ref, k_ref, vt_ref, *rest, tq, kvh, rep, lk, ck, cache_len):
    if cache_len:
        ck_ref, cvt_ref, o_ref, s_scr = rest
    else:
        o_ref, s_scr = rest
    steps = []
    for j in range(kvh):
        steps += [(j, c, False) for c in range(lk // ck)]
        if cache_len:
            steps.append((j, 0, True))
    per_head = len(steps) // kvh
    qs = [jnp.concatenate([q_ref[:, (j * rep + g) * HEAD_DIM:(j * rep + g + 1) * HEAD_DIM] for g in range(rep)],
                          axis=0) for j in range(kvh)]

    def logits(t):
        j, c, cached = steps[t]
        lanes = slice(j * HEAD_DIM, (j + 1) * HEAD_DIM)
        kc = ck_ref[:, lanes] if cached else k_ref[c * ck:(c + 1) * ck, lanes]
        st = _dot_nt(kc.astype(BF16), qs[j])
        s_scr[t % 2, 0:st.shape[0], :] = st
        return jnp.max(st, axis=0, keepdims=True)

    cmax = logits(0)
    m = acc = None
    heads_t = []
    for t, (j, c, cached) in enumerate(steps):
        nxt = logits(t + 1) if t + 1 < len(steps) else None
        n_keys = cache_len if cached else ck
        vrows = slice(j * VT_ROWS, (j + 1) * VT_ROWS)
        vt = cvt_ref[vrows, :] if cached else vt_ref[vrows, c * ck:(c + 1) * ck]
        st = s_scr[t % 2, 0:n_keys, :]
        if t % per_head == 0:
            m = cmax
            acc = _dot(vt, jnp.exp2(st - m).astype(BF16))
        else:
            m_new = jnp.maximum(m, cmax)
            acc = jnp.exp2(m - m_new) * acc + _dot(vt, jnp.exp2(st - m_new).astype(BF16))
            m = m_new
        if t % per_head == per_head - 1:
            o = acc[0:HEAD_DIM] / acc[HEAD_DIM:HEAD_DIM + 1]
            heads_t += [o[:, g * tq:(g + 1) * tq] for g in range(rep)]
        cmax = nxt
    o_ref[...] = jnp.concatenate(heads_t, axis=0).T


def _vt_layout(v, batch, seq, kvh):
    vt = v.reshape(batch, seq, kvh, HEAD_DIM).transpose(0, 2, 3, 1).astype(BF16)
    ones = jnp.ones((batch, kvh, VT_ROWS - HEAD_DIM, seq), BF16)
    return jnp.concatenate([vt, ones], axis=2).reshape(batch, kvh * VT_ROWS, seq)


def _block_attention(q, k, vt, cache_k, cache_vt, *, batch, seq, kvh, rep, tq, ck, name):
    n = batch * seq
    wkv = kvh * HEAD_DIM
    nq = seq // tq
    cache_len = 0 if cache_k is None else cache_k.shape[1]
    in_specs = [
        pl.BlockSpec((tq, BRANCH_W), lambda b, i: (b * nq + i, 0)),
        pl.BlockSpec((None, seq, wkv), lambda b, i: (b, 0, 0)),
        pl.BlockSpec((None, kvh * VT_ROWS, seq), lambda b, i: (b, 0, 0)),
    ]
    args = [q, k.reshape(batch, seq, wkv), vt]
    if cache_len:
        in_specs += [pl.BlockSpec((None, cache_len, wkv), lambda b, i: (b, 0, 0)),
                     pl.BlockSpec((None, kvh * VT_ROWS, cache_len), lambda b, i: (b, 0, 0))]
        args += [cache_k, cache_vt]
    return pl.pallas_call(
        functools.partial(_block_attn_kernel, tq=tq, kvh=kvh, rep=rep, lk=seq, ck=ck, cache_len=cache_len),
        grid=(batch, nq),
        in_specs=in_specs,
        out_specs=pl.BlockSpec((tq, BRANCH_W), lambda b, i: (b * nq + i, 0)),
        out_shape=jax.ShapeDtypeStruct((n, BRANCH_W), F32),
        scratch_shapes=[pltpu.VMEM((2, max(ck, cache_len), rep * tq), F32)],
        compiler_params=_params("parallel", "arbitrary"),
        name=name,
    )(*args)


NA_QROWS = 4
NA_KROWS = 12
NA_TQ = NA_QROWS * GRID_W
NA_TK = NA_KROWS * GRID_W
NA_BLOCKS_PER_STEP = 2


def _na_kernel(q_ref, k_ref, vt_ref, ck_ref, cvt_ref, bias_ref, o_ref, s_scr, *, rows, cache_len):
    i = pl.program_id(1)
    nq = rows // NA_QROWS
    r0s, kinds = [], []
    for u in range(NA_BLOCKS_PER_STEP):
        qi = i * NA_BLOCKS_PER_STEP + u
        kbase = jnp.clip(qi * NA_QROWS - NA_KH // 2, 0, rows - NA_KROWS)
        r0s.append(pl.multiple_of(kbase * GRID_W, 2 * GRID_W))
        kinds.append(jnp.where(qi == 0, 0, jnp.where(qi == nq - 1, 2, 1)))
    steps = [(u, h) for u in range(NA_BLOCKS_PER_STEP) for h in range(NA_HEADS)]

    def logits(t):
        u, h = steps[t]
        lanes = slice(h * HEAD_DIM, (h + 1) * HEAD_DIM)
        qh = q_ref[u * NA_TQ:(u + 1) * NA_TQ, lanes]
        s_win = _dot_nt(k_ref[pl.ds(r0s[u], NA_TK), lanes], qh) + bias_ref[kinds[u], h]
        s_ctx = _dot_nt(ck_ref[:, lanes], qh)
        s_scr[t % 2, 0:NA_TK, :] = s_win
        s_scr[t % 2, NA_TK:NA_TK + cache_len, :] = s_ctx
        return jnp.maximum(jnp.max(s_win, axis=0, keepdims=True), jnp.max(s_ctx, axis=0, keepdims=True))

    m = logits(0)
    heads_t = []
    for t, (u, h) in enumerate(steps):
        nxt = logits(t + 1) if t + 1 < len(steps) else None
        vrows = slice(h * VT_ROWS, (h + 1) * VT_ROWS)
        p_win = jnp.exp2(s_scr[t % 2, 0:NA_TK, :] - m).astype(BF16)
        p_ctx = jnp.exp2(s_scr[t % 2, NA_TK:NA_TK + cache_len, :] - m).astype(BF16)
        acc = _dot(vt_ref[vrows, pl.ds(r0s[u], NA_TK)], p_win) + _dot(cvt_ref[vrows, :], p_ctx)
        heads_t.append(acc[0:HEAD_DIM] / acc[HEAD_DIM:HEAD_DIM + 1])
        if h == NA_HEADS - 1:
            o_ref[u * NA_TQ:(u + 1) * NA_TQ, :] = jnp.concatenate(heads_t, axis=0).T
            heads_t = []
        m = nxt


def _na_attention(q, k, vt, cache_k, cache_vt, bias, *, batch, seq):
    n = batch * seq
    rows = seq // GRID_W
    nq = rows // NA_QROWS
    cache_len = cache_k.shape[1]
    vt_rows = NA_HEADS * VT_ROWS

    nstep = nq // NA_BLOCKS_PER_STEP
    tq = NA_BLOCKS_PER_STEP * NA_TQ
    return pl.pallas_call(
        functools.partial(_na_kernel, rows=rows, cache_len=cache_len),
        grid=(batch, nstep),
        in_specs=[
            pl.BlockSpec((tq, BRANCH_W), lambda b, i: (b * nstep + i, 0)),
            pl.BlockSpec((None, seq, BRANCH_W), lambda b, i: (b, 0, 0)),
            pl.BlockSpec((None, vt_rows, seq), lambda b, i: (b, 0, 0)),
            pl.BlockSpec((None, cache_len, BRANCH_W), lambda b, i: (b, 0, 0)),
            pl.BlockSpec((None, vt_rows, cache_len), lambda b, i: (b, 0, 0)),
            _full((3, NA_HEADS, NA_TK, NA_TQ)),
        ],
        out_specs=pl.BlockSpec((tq, BRANCH_W), lambda b, i: (b * nstep + i, 0)),
        out_shape=jax.ShapeDtypeStruct((n, BRANCH_W), F32),
        scratch_shapes=[pltpu.VMEM((2, NA_TK + cache_len, NA_TQ), F32)],
        compiler_params=_params("parallel", "arbitrary"),
        name="na_latent",
    )(q, k.reshape(batch, seq, BRANCH_W), vt, cache_k, cache_vt, bias)


def _na_bias_tables(rel_bias, rows):
    nq = rows // NA_QROWS
    kh = min(NA_KH, rows)
    n_dr, n_dc = 2 * NA_KH - 1, 2 * NA_KW - 1
    cq = np.arange(GRID_W)
    start_c = np.clip(cq - NA_KW // 2, 0, GRID_W - NA_KW)
    col_ok = (cq[None, :] >= start_c[:, None]) & (cq[None, :] < start_c[:, None] + NA_KW)
    dc = np.clip(cq[None, :] - cq[:, None], -(NA_KW - 1), NA_KW - 1) + (NA_KW - 1)
    onehot = (dc.T.reshape(1, -1) == np.arange(n_dc)[:, None]).astype(np.float32)
    per_dr = jnp.einsum("hrc,cx->hrx", rel_bias.astype(F32), jnp.asarray(onehot), precision=lax.Precision.HIGHEST)
    per_dr = jnp.where(col_ok.T.reshape(1, 1, -1), per_dr * math.log2(math.e), NEG_INF)
    masked = jnp.full((NA_HEADS, 1, GRID_W * GRID_W), NEG_INF, F32)
    per_dr = jnp.concatenate([per_dr, masked], axis=1).reshape(NA_HEADS, n_dr + 1, GRID_W, GRID_W)
    tabs = []
    for blk in (0, 1, nq - 1):
        kbase = int(np.clip(blk * NA_QROWS - NA_KH // 2, 0, rows - NA_KROWS))
        r = blk * NA_QROWS + np.arange(NA_QROWS)
        start_r = np.clip(r - kh // 2, 0, rows - kh)
        kr = kbase + np.arange(NA_KROWS)
        row_ok = (kr[None, :] >= start_r[:, None]) & (kr[None, :] < start_r[:, None] + kh)
        dr = np.where(row_ok, kr[None, :] - r[:, None] + (NA_KH - 1), n_dr)
        tabs.append(jnp.concatenate(
            [jnp.concatenate([per_dr[:, int(dr[a, b])] for a in range(NA_QROWS)], axis=-1)
             for b in range(NA_KROWS)], axis=-2))
    return jnp.stack(tabs, axis=0)


def _gelu_tanh(y):
    return 0.5 * y * (1.0 + jnp.tanh(math.sqrt(2.0 / math.pi) * (y + 0.044715 * (y * y * y))))


S5_PARTS = 1


def _aligned(offset, multiple):
    return offset if isinstance(offset, int) else pl.multiple_of(offset, multiple)


def _s5_kernel(x0_ref, xn_ref, bmat_ref, cmat_ref, abar_ref, s0_ref, *rest, steps, reverse, nchunk):
    if reverse:
        xprev_ref, yprev_ref, dvec_ref, wglu_ref, bglu_ref, y_ref, sfin_ref, bu0, bu1, sb0, sb1, st_ref = rest
    else:
        y_ref, sfin_ref, bu0, bu1, sb0, sb1, st_ref = rest
    j = pl.program_id(1)

    @pl.when(j == 0)
    def _():
        st_ref[...] = s0_ref[...]
        sb0[...] = jnp.zeros_like(sb0)
        bu0[...] = _dot(x0_ref[...].astype(BF16), bmat_ref[...])

    pair_rows = 2 * SUBLANES
    part_rows = steps * SUBLANES // S5_PARTS
    pairs = part_rows // pair_rows

    def stage(bu_cur, bu_nxt, sb_prev, sb_cur):
        a_re = jnp.broadcast_to(abar_ref[:, 0:SSM_N], (SUBLANES, SSM_N))
        a_im = jnp.broadcast_to(abar_ref[:, SSM_N:], (SUBLANES, SSM_N))
        old_re, old_im = st_ref[:, 0:SSM_N], st_ref[:, SSM_N:]

        def part(q, carry):
            s_re, s_im = carry
            mrows = pl.ds(_aligned(q * part_rows, part_rows), part_rows)
            y = _dot(sb_prev[mrows, :], cmat_ref[...])
            bu_nxt[mrows, :] = _dot(xn_ref[mrows, :].astype(BF16), bmat_ref[...])
            if reverse:
                y = yprev_ref[mrows, :] + y + dvec_ref[...] * xprev_ref[mrows, :]
                y = _gelu_tanh(y)
                y = y * jax.nn.sigmoid(_dot(y.astype(BF16), wglu_ref[...]) + bglu_ref[...])
            y_ref[mrows, :] = y

            base = ((S5_PARTS - 1 - q) if reverse else q) * part_rows
            for i in range(pairs):
                r0 = _aligned(base + ((pairs - 1 - i) if reverse else i) * pair_rows, pair_rows)
                out_re, out_im = [None, None], [None, None]
                for half in ((1, 0) if reverse else (0, 1)):
                    rr = _aligned(r0 + half * SUBLANES, SUBLANES)
                    n_re = a_re * s_re - a_im * s_im + bu_cur[pl.ds(rr, SUBLANES), 0:SSM_N]
                    n_im = a_re * s_im + a_im * s_re + bu_cur[pl.ds(rr, SUBLANES), SSM_N:]
                    out_re[half], out_im[half] = n_re, n_im
                    s_re, s_im = n_re, n_im
                sb_cur[pl.ds(r0, pair_rows), 0:SSM_N] = jnp.concatenate(out_re, axis=0).astype(BF16)
                sb_cur[pl.ds(r0, pair_rows), SSM_N:] = jnp.concatenate(out_im, axis=0).astype(BF16)
            return s_re, s_im

        if S5_PARTS == 1:
            s_re, s_im = part(0, (old_re, old_im))
        else:
            s_re, s_im = lax.fori_loop(0, S5_PARTS, part, (old_re, old_im))
        live = j < nchunk
        st_ref[:, 0:SSM_N] = jnp.where(live, s_re, old_re)
        st_ref[:, SSM_N:] = jnp.where(live, s_im, old_im)

    @pl.when(j % 2 == 0)
    def _():
        stage(bu0, bu1, sb0, sb1)

    @pl.when(j % 2 == 1)
    def _():
        stage(bu1, bu0, sb1, sb0)

    sfin_ref[...] = st_ref[...]


def _s5_pass(x3, sp, s0, yprev, lw, *, steps, reverse):
    halves, rows, _ = x3.shape
    nchunk = rows // (steps * SUBLANES)
    tr = steps * SUBLANES

    def chunk_of(step):
        step = jnp.clip(step, 0, nchunk - 1)
        return (nchunk - 1 - step) if reverse else step

    first = lambda h, j: (h, chunk_of(0), 0)
    nxt = lambda h, j: (h, chunk_of(j + 1), 0)
    prev = lambda h, j: (h, chunk_of(j - 1), 0)

    in_specs = [
        pl.BlockSpec((None, tr, BRANCH_W), first),
        pl.BlockSpec((None, tr, BRANCH_W), nxt),
        _full((BRANCH_W, 2 * SSM_N)),
        _full((2 * SSM_N, BRANCH_W)),
        _full((1, 2 * SSM_N)),
        pl.BlockSpec((None, SUBLANES, 2 * SSM_N), lambda h, j: (h, 0, 0)),
    ]
    args = [x3, x3, sp["bmat"], sp["cmat"], sp["abar"], s0]
    if reverse:
        in_specs += [pl.BlockSpec((None, tr, BRANCH_W), prev), pl.BlockSpec((None, tr, BRANCH_W), prev),
                     _full((1, BRANCH_W)), _full((BRANCH_W, BRANCH_W)), _full((1, BRANCH_W))]
        args += [x3, yprev, lw["ssm_d"], lw["w_glu"], lw["b_glu"]]
    return pl.pallas_call(
        functools.partial(_s5_kernel, steps=steps, reverse=reverse, nchunk=nchunk),
        grid=(halves, nchunk + 1),
        in_specs=in_specs,
        out_specs=[pl.BlockSpec((None, tr, BRANCH_W), prev),
                   pl.BlockSpec((None, SUBLANES, 2 * SSM_N), lambda h, j: (h, 0, 0))],
        out_shape=[jax.ShapeDtypeStruct(x3.shape, F32),
                   jax.ShapeDtypeStruct((halves, SUBLANES, 2 * SSM_N), F32)],
        scratch_shapes=[pltpu.VMEM((tr, 2 * SSM_N), F32), pltpu.VMEM((tr, 2 * SSM_N), F32),
                        pltpu.VMEM((tr, 2 * SSM_N), BF16), pltpu.VMEM((tr, 2 * SSM_N), BF16),
                        pltpu.VMEM((SUBLANES, 2 * SSM_N), F32)],
        compiler_params=_params("parallel", "arbitrary"),
        name="s5_reverse" if reverse else "s5_forward",
    )(*args)


def _s5_discretise(a_re, a_im, log_dt, b_re, b_im, c_re, c_im):
    dt = jnp.exp(log_dt)[:, None]
    mag = jnp.exp(dt * a_re)
    ab_re, ab_im = mag * jnp.cos(dt * a_im), mag * jnp.sin(dt * a_im)
    den = a_re * a_re + a_im * a_im
    nr, ni = ab_re - 1.0, ab_im
    coef_re = ((nr * a_re + ni * a_im) / den)[..., None]
    coef_im = ((ni * a_re - nr * a_im) / den)[..., None]
    bb_re = coef_re * b_re - coef_im * b_im
    bb_im = coef_re * b_im + coef_im * b_re
    eye = jnp.eye(SSM_G, dtype=F32)

    def in_blocks(bb):
        return jnp.einsum("gpi,gk->gikp", bb, eye).reshape(SSM_G * SSM_H, SSM_N)

    def out_blocks(cc):
        return jnp.einsum("gip,gk->gpki", cc, eye).reshape(SSM_N, SSM_G * SSM_H)

    bmat = jnp.concatenate([in_blocks(bb_re), in_blocks(bb_im)], axis=1).astype(BF16)
    cmat = jnp.concatenate([out_blocks(c_re), -out_blocks(c_im)], axis=0).astype(BF16)
    abar = jnp.concatenate([ab_re.reshape(1, SSM_N), ab_im.reshape(1, SSM_N)], axis=1)
    return dict(bmat=bmat, cmat=cmat, abar=abar)


def _state_to_rows(s):
    b = s.shape[0]
    flat = jnp.concatenate([s[..., 0].reshape(b, SSM_N), s[..., 1].reshape(b, SSM_N)], axis=1)
    return flat.reshape(b // SUBLANES, SUBLANES, 2 * SSM_N)


def _rows_to_state(r):
    b = r.shape[0] * SUBLANES
    flat = r.reshape(b, 2, SSM_G, SSM_P)
    return jnp.stack([flat[:, 0], flat[:, 1]], axis=-1)


MERGE_SUB_ROWS = 256


def _merge_kernel(x_ref, g_ref, scale_ref, shift_ref, gate_ref, ya_ref, yb_ref, yc_ref, yd_ref,
                  wz_ref, wg_ref, wbr_ref, wout_ref, fg_ref, o_ref, *, last):
    tm = x_ref.shape[0]
    sub = min(tm, MERGE_SUB_ROWS)
    for r0 in range(0, tm, sub):
        rows = slice(r0, r0 + sub)
        x = x_ref[rows, :]
        hb = _modulated_norm(x, g_ref[...], scale_ref[...], shift_ref[...]).astype(BF16)
        z = _dot(hb, wz_ref[...])
        merged = None
        for n, y_ref in enumerate((ya_ref, yb_ref, yc_ref, yd_ref)):
            zn = z[:, n * BRANCH_W:(n + 1) * BRANCH_W]
            yn = y_ref[rows, :] * (zn * jax.nn.sigmoid(zn))
            t = _dot(yn.astype(BF16), wbr_ref[n * BRANCH_W:(n + 1) * BRANCH_W, :])
            gn = _dot(hb, wg_ref[:, n * D_MODEL:(n + 1) * D_MODEL])
            term = jax.nn.sigmoid(gn) * t
            merged = term if merged is None else merged + term
        out = x + gate_ref[...] * _dot(merged.astype(BF16), wout_ref[...])
        if last:
            out = (out * lax.rsqrt(jnp.mean(out * out, axis=-1, keepdims=True) + EPS)) * fg_ref[...]
        o_ref[rows, :] = out


def _merge(x, mod_scale, mod_shift, mod_gate, ya, yb, yc3, yd, lw, final_g, *, batch, seq, tm, last, name):
    n = batch * seq
    tiles_per_seq = seq // tm
    per_batch_mod = mod_scale.shape[0] > 1

    def mod_idx(i):
        return ((i // tiles_per_seq) if per_batch_mod else 0, 0, 0)

    def yc_idx(i):
        b = i // tiles_per_seq
        return (b // SUBLANES, i % tiles_per_seq, b % SUBLANES)

    row = lambda i: (i, 0)
    ytile = pl.BlockSpec((tm, BRANCH_W), row)
    return pl.pallas_call(
        functools.partial(_merge_kernel, last=last),
        grid=(n // tm,),
        in_specs=[
            pl.BlockSpec((tm, D_MODEL), row),
            _full((1, D_MODEL)),
            pl.BlockSpec((None, 1, D_MODEL), mod_idx),
            pl.BlockSpec((None, 1, D_MODEL), mod_idx),
            pl.BlockSpec((None, 1, D_MODEL), mod_idx),
            ytile, ytile,
            pl.BlockSpec((None, tm, BRANCH_W), yc_idx),
            ytile,
            _full((D_MODEL, 4 * BRANCH_W)),
            _full((D_MODEL, 4 * D_MODEL)),
            _full((4 * BRANCH_W, D_MODEL)),
            _full((D_MODEL, D_MODEL)),
            _full((1, D_MODEL)),
        ],
        out_specs=pl.BlockSpec((tm, D_MODEL), row),
        out_shape=jax.ShapeDtypeStruct((n, D_MODEL), F32),
        compiler_params=_params("parallel"),
        name=name,
    )(x, lw["norm_g"], mod_scale, mod_shift, mod_gate, ya, yb, yc3, yd,
      lw["w_z"], lw["w_g"], lw["w_branch"], lw["w_out"], final_g)


def _rope_tables(seq):
    t = np.arange(seq)
    row = (t // GRID_W).astype(np.float32)
    col = (t % GRID_W).astype(np.float32)
    nf = HEAD_DIM // 4
    freqs = jnp.asarray(ROPE_BASE, F32) ** (-jnp.arange(nf, dtype=F32) / nf)
    ang_r = jnp.asarray(row)[:, None] * freqs[None, :]
    ang_c = jnp.asarray(col)[:, None] * freqs[None, :]
    cos = jnp.concatenate([jnp.cos(ang_r)] * 2 + [jnp.cos(ang_c)] * 2, axis=1)
    sin = jnp.concatenate([-jnp.sin(ang_r), jnp.sin(ang_r), -jnp.sin(ang_c), jnp.sin(ang_c)], axis=1)
    return jnp.tile(cos, (1, GQA_HEADS)), jnp.tile(sin, (1, GQA_HEADS))


def _layer_weights(l, norm_g, w_in, gm_v_g, gm_ws, gm_b, ssm_d, w_glu, b_glu, gqa_q_g, gqa_k_g, w_branch, w_out):
    w = w_in[l]
    w_proj = jnp.concatenate([w[:, 0:512], w[:, 768:1536], w[:, 1792:2048], w[:, 2304:2816]], axis=1).astype(BF16)
    w_z = jnp.concatenate([w[:, 512:768], w[:, 1536:1792], w[:, 2048:2304], w[:, 2816:3072]], axis=1).astype(BF16)
    return dict(
        norm_g=norm_g[l].reshape(1, D_MODEL),
        w_proj=w_proj,
        w_z=w_z,
        w_g=w[:, 3072:].astype(BF16),
        gm_v_g=gm_v_g[l].reshape(1, BRANCH_W),
        gm_w=gm_ws[l].reshape(GM_GROUPS * CHUNK, CHUNK).astype(BF16),
        gm_bias=jnp.repeat(gm_b[l].T, HEAD_DIM, axis=1),
        ssm_d=ssm_d[l].reshape(1, BRANCH_W),
        w_glu=w_glu[l].astype(BF16),
        b_glu=b_glu[l].reshape(1, BRANCH_W),
        gqa_q_g=jnp.tile(gqa_q_g[l], GQA_HEADS).reshape(1, BRANCH_W),
        gqa_k_g=jnp.tile(gqa_k_g[l], GQA_KV).reshape(1, 128),
        w_branch=w_branch[l].reshape(4 * BRANCH_W, D_MODEL).astype(BF16),
        w_out=w_out[l].astype(BF16),
    )


def _trunk_layer(x, mods, lw, sps, consts, final_g, cache, *, batch, seq, latent, last):
    scale, shift, gate = mods
    tm = 512 if latent else 256
    ya, bq, bk, bv, cx, dq, dk, dv = _inproj(x, scale, shift, lw, consts, batch=batch, seq=seq, tm=tm, latent=latent)
    halves = batch // SUBLANES
    cx3 = cx.reshape(halves, seq * SUBLANES, BRANCH_W)
    if latent:
        yb = _na_attention(bq, bk, bv, cache["na_k"], cache["na_vt"], cache["na_bias"], batch=batch, seq=seq)
        yd = _block_attention(dq, dk, dv, cache["gqa_k"], cache["gqa_vt"], batch=batch, seq=seq,
                              kvh=GQA_KV, rep=GQA_HEADS // GQA_KV, tq=256, ck=512, name="gqa_latent")
        s0 = cache["ssm"]
    else:
        yb = _block_attention(bq, bk, _vt_layout(bv, batch, seq, NA_HEADS), None, None, batch=batch, seq=seq,
                              kvh=NA_HEADS, rep=1, tq=seq, ck=seq, name="na_ctx")
        yd = _block_attention(dq, dk, _vt_layout(dv, batch, seq, GQA_KV), None, None, batch=batch, seq=seq,
                              kvh=GQA_KV, rep=GQA_HEADS // GQA_KV, tq=seq, ck=seq, name="gqa_ctx")
        zero = jnp.zeros((halves, SUBLANES, 2 * SSM_N), F32)
        s0 = (zero, zero)
    steps = 64
    yf, sf = _s5_pass(cx3, sps[0], s0[0], None, lw, steps=steps, reverse=False)
    yc3, sr = _s5_pass(cx3, sps[1], s0[1], yf, lw, steps=steps, reverse=True)
    yc3 = yc3.reshape(halves, seq, SUBLANES * BRANCH_W)
    x_new = _merge(x, scale, shift, gate, ya, yb, yc3, yd, lw, final_g, batch=batch, seq=seq, tm=tm, last=last,
                   name="merge_latent" if latent else "merge_ctx")
    return x_new, (bk, bv, dk, dv, sf, sr)


def kernel(x_prompt, x_sample, c, cache_na_k, cache_na_v, cache_gqa_k, cache_gqa_v, state_ssm, c_ctx,
           norm_g, w_ada, b_ada, w_in, gm_v_g, gm_ws, gm_b, na_rel_bias, ssm_a_re, ssm_a_im, ssm_log_dt,
           ssm_b_re, ssm_b_im, ssm_c_re, ssm_c_im, ssm_d, w_glu, b_glu, gqa_q_g, gqa_k_g, w_branch, w_out,
           final_g):
    bc, lc, _ = x_prompt.shape
    bl, ll, _ = x_sample.shape
    past = cache_na_k.shape[2]

    n_rows = 16
    cond = jnp.zeros((n_rows, D_MODEL), F32).at[0].set(c_ctx).at[1:1 + bl].set(c)
    mod = _modulation(cond, w_ada, b_ada)

    def mods_of(l, lo, hi):
        m = mod[l, lo:hi].reshape(hi - lo, 1, 3 * D_MODEL)
        return m[..., 0:D_MODEL], m[..., D_MODEL:2 * D_MODEL], m[..., 2 * D_MODEL:]

    cos, sin = _rope_tables(ll)
    ones_blk = jnp.asarray(np.kron(np.eye(BRANCH_W // HEAD_DIM), np.ones((HEAD_DIM, HEAD_DIM))), BF16)
    consts = dict(rope_cos=cos, rope_sin=sin, ones_blk=ones_blk)
    fg = final_g.reshape(1, D_MODEL)

    xp = x_prompt.reshape(bc * lc, D_MODEL)
    xs = x_sample.reshape(bl * ll, D_MODEL)
    na_k_l, na_v_l, gqa_k_l, gqa_v_l, ssm_l = [], [], [], [], []
    for l in range(DEPTH):
        lw = _layer_weights(l, norm_g, w_in, gm_v_g, gm_ws, gm_b, ssm_d, w_glu, b_glu, gqa_q_g, gqa_k_g,
                            w_branch, w_out)
        sps = [_s5_discretise(ssm_a_re[l, d], ssm_a_im[l, d], ssm_log_dt[l, d], ssm_b_re[l, d], ssm_b_im[l, d],
                              ssm_c_re[l, d], ssm_c_im[l, d]) for d in range(2)]
        last = l == DEPTH - 1
        shift, scale, gate = mods_of(l, 0, 1)
        xp, (k_na, v_na, k_g, v_g, sf, sr) = _trunk_layer(
            xp, (scale, shift, gate), lw, sps, consts, fg, None, batch=bc, seq=lc, latent=False, last=last)
        na_k_l.append(k_na.reshape(bc, lc, NA_HEADS, HEAD_DIM))
        na_v_l.append(v_na.reshape(bc, lc, NA_HEADS, HEAD_DIM))
        gqa_k_l.append(k_g.reshape(bc, lc, GQA_KV, HEAD_DIM))
        gqa_v_l.append(v_g.reshape(bc, lc, GQA_KV, HEAD_DIM))
        ssm_l.append(jnp.stack([_rows_to_state(sf), _rows_to_state(sr)], axis=1))
        shift, scale, gate = mods_of(l, 1, 1 + bl)
        cache = dict(
            na_k=cache_na_k[:, l].reshape(bl, past, BRANCH_W).astype(BF16),
            na_vt=_vt_layout(cache_na_v[:, l], bl, past, NA_HEADS),
            gqa_k=cache_gqa_k[:, l].reshape(bl, past, GQA_KV * HEAD_DIM).astype(BF16),
            gqa_vt=_vt_layout(cache_gqa_v[:, l], bl, past, GQA_KV),
            ssm=(_state_to_rows(state_ssm[:, l, 0]), _state_to_rows(state_ssm[:, l, 1])),
            na_bias=_na_bias_tables(na_rel_bias[l], ll // GRID_W),
        )
        xs, _ = _trunk_layer(xs, (scale, shift, gate), lw, sps, consts, fg, cache,
                             batch=bl, seq=ll, latent=True, last=last)
    return (xp.reshape(bc, lc, D_MODEL), xs.reshape(bl, ll, D_MODEL),
            jnp.stack(na_k_l, axis=1), jnp.stack(na_v_l, axis=1),
            jnp.stack(gqa_k_l, axis=1), jnp.stack(gqa_v_l, axis=1), jnp.stack(ssm_l, axis=1))
```

```python
import functools
import math

import numpy as np
import jax
import jax.numpy as jnp
from jax import lax
from jax.experimental import pallas as pl
from jax.experimental.pallas import tpu as pltpu

D_MODEL = 1024
DEPTH = 2
GRID_W = 64
BRANCH_W = 256
HEAD_DIM = 64
CHUNK = 128
GM_GROUPS = 4
NA_HEADS = 4
NA_KH = 8
NA_KW = 16
SSM_H = 16
SSM_G = 16
SSM_P = 64
SSM_N = SSM_G * SSM_P
GQA_HEADS = 4
GQA_KV = 2
ROPE_BASE = 10000.0
EPS = 1e-6
NEG_INF = -1e30
ATTN_SCALE = HEAD_DIM ** -0.5

SUBLANES = 8
VMEM_LIMIT = 56 * 1024 * 1024

F32 = jnp.float32
BF16 = jnp.bfloat16


def _params(*sem):
    return pltpu.CompilerParams(dimension_semantics=sem, vmem_limit_bytes=VMEM_LIMIT)


def _dot(a, b):
    return jnp.dot(a, b, preferred_element_type=F32)


def _dot_nt(a, b):
    return lax.dot_general(a, b, (((1,), (1,)), ((), ())), preferred_element_type=F32)


def _full(shape):
    nd = len(shape)
    return pl.BlockSpec(shape, lambda *_: (0,) * nd)


def _modulated_norm(x, g, scale, shift):
    y = x * lax.rsqrt(jnp.mean(x * x, axis=-1, keepdims=True) + EPS)
    return (y * g) * (1.0 + scale) + shift


def _group_sumsq(x, ones_blk):
    x2 = x * x
    hi = x2.astype(BF16)
    lo = (x2 - hi.astype(F32)).astype(BF16)
    return _dot(hi, ones_blk) + _dot(lo, ones_blk)


def _head_rmsnorm(x, ones_blk, g):
    ss = _group_sumsq(x, ones_blk)
    return (x * lax.rsqrt(ss * (1.0 / HEAD_DIM) + EPS)) * g


def _rope(x, cos, sin_signed):
    w = x.shape[1]
    lane = lax.broadcasted_iota(jnp.int32, x.shape, 1)
    up = pltpu.roll(x, 16, axis=1)
    dn = pltpu.roll(x, w - 16, axis=1)
    partner = jnp.where((lane & 16) != 0, up, dn)
    return x * cos + partner * sin_signed


VT_ROWS = HEAD_DIM + 16
QK_SCALE_LOG2 = ATTN_SCALE * math.log2(math.e)


def _vt_tile(v, heads):
    vt = v.T.astype(BF16)
    ones = jnp.ones((VT_ROWS - HEAD_DIM, v.shape[0]), BF16)
    parts = []
    for h in range(heads):
        parts += [vt[h * HEAD_DIM:(h + 1) * HEAD_DIM], ones]
    return jnp.concatenate(parts, axis=0)


def _mod_kernel(cond_ref, w_ref, b_ref, o_ref):
    c = cond_ref[...]
    s = (c * jax.nn.sigmoid(c)).astype(BF16)
    o_ref[...] = _dot(s, w_ref[...].astype(BF16)) + b_ref[...]


def _modulation(cond, w_ada, b_ada):
    r = cond.shape[0]
    nj = 3
    return pl.pallas_call(
        _mod_kernel,
        grid=(DEPTH, nj),
        in_specs=[
            pl.BlockSpec((r, D_MODEL), lambda l, j: (0, 0)),
            pl.BlockSpec((None, D_MODEL, D_MODEL), lambda l, j: (l, 0, j)),
            pl.BlockSpec((None, 1, D_MODEL), lambda l, j: (l, 0, j)),
        ],
        out_specs=pl.BlockSpec((None, r, D_MODEL), lambda l, j: (l, 0, j)),
        out_shape=jax.ShapeDtypeStruct((DEPTH, r, 3 * D_MODEL), F32),
        compiler_params=_params("arbitrary", "arbitrary"),
        name="adaln_modulation",
    )(cond, w_ada, b_ada.reshape(DEPTH, 1, 3 * D_MODEL))


INPROJ_SUB_ROWS = 256


def _inproj_kernel(x_ref, g_ref, scale_ref, shift_ref, w_ref, gmg_ref, gmw_ref, gmb_ref, ones_ref,
                   qg_ref, kg_ref, *rest, tm, latent):
    if latent:
        cos_ref, sin_ref = rest[:2]
        rest = rest[2:]
    ya_ref, bq_ref, bk_ref, bv_ref, cx_ref, dq_ref, dk_ref, dv_ref = rest

    lane_grp = lax.broadcasted_iota(jnp.int32, (CHUNK, BRANCH_W), 1) // HEAD_DIM
    ones_blk = ones_ref[...]
    sub = min(tm, INPROJ_SUB_ROWS)
    for r0 in range(0, tm, sub):
        rows = slice(r0, r0 + sub)
        h = _modulated_norm(x_ref[rows, :], g_ref[...], scale_ref[...], shift_ref[...])
        p = _dot(h.astype(BF16), w_ref[...])

        u = p[:, 0:256]
        v = p[:, 256:512]
        vn = (v * lax.rsqrt(jnp.mean(v * v, axis=-1, keepdims=True) + EPS)) * gmg_ref[...]
        vnb = vn.astype(BF16)
        for c in range(sub // CHUNK):
            r = _dot(gmw_ref[...], vnb[c * CHUNK:(c + 1) * CHUNK, :])
            sp = r[0:CHUNK]
            for g in range(1, GM_GROUPS):
                sp = jnp.where(lane_grp == g, r[g * CHUNK:(g + 1) * CHUNK], sp)
            sp = sp + gmb_ref[...]
            ya_ref[r0 + c * CHUNK:r0 + (c + 1) * CHUNK, :] = u[c * CHUNK:(c + 1) * CHUNK, :] * sp

        bq_ref[rows, :] = (p[:, 512:768] * QK_SCALE_LOG2).astype(bq_ref.dtype)
        bk_ref[rows, :] = p[:, 768:1024].astype(bk_ref.dtype)
        if latent:
            bv_ref[:, rows] = _vt_tile(p[:, 1024:1280], NA_HEADS)
        else:
            bv_ref[rows, :] = p[:, 1024:1280]
        cx_ref[rows, :] = p[:, 1280:1536]
        dq = _head_rmsnorm(p[:, 1536:1792], ones_blk, qg_ref[...])
        dk = _head_rmsnorm(p[:, 1792:1920], ones_blk[0:128, 0:128], kg_ref[...])
        if latent:
            cos = cos_ref[rows, :]
            sin = sin_ref[rows, :]
            dq = _rope(dq, cos, sin)
            dk = _rope(dk, cos[:, 0:128], sin[:, 0:128])
        dq_ref[rows, :] = (dq * QK_SCALE_LOG2).astype(dq_ref.dtype)
        dk_ref[rows, :] = dk.astype(dk_ref.dtype)
        if latent:
            dv_ref[:, rows] = _vt_tile(p[:, 1920:2048], GQA_KV)
        else:
            dv_ref[rows, :] = p[:, 1920:2048]


def _inproj(x, mod_scale, mod_shift, lw, consts, *, batch, seq, tm, latent):
    n = batch * seq
    tiles_per_seq = seq // tm
    halves = batch // SUBLANES
    per_batch_mod = mod_scale.shape[0] > 1

    def mod_idx(i):
        return ((i // tiles_per_seq) if per_batch_mod else 0, 0, 0)

    row = lambda i: (i, 0)
    kv_dtype = BF16 if latent else F32
    in_specs = [
        pl.BlockSpec((tm, D_MODEL), row),
        _full((1, D_MODEL)),
        pl.BlockSpec((None, 1, D_MODEL), mod_idx),
        pl.BlockSpec((None, 1, D_MODEL), mod_idx),
        _full((D_MODEL, 2048)),
        _full((1, BRANCH_W)),
        _full((GM_GROUPS * CHUNK, CHUNK)),
        _full((CHUNK, BRANCH_W)),
        _full((BRANCH_W, BRANCH_W)),
        _full((1, BRANCH_W)),
        _full((1, 128)),
    ]
    args = [x, lw["norm_g"], mod_scale, mod_shift, lw["w_proj"], lw["gm_v_g"], lw["gm_w"], lw["gm_bias"],
            consts["ones_blk"], lw["gqa_q_g"], lw["gqa_k_g"]]
    if latent:
        rope_idx = lambda i: (i % tiles_per_seq, 0)
        in_specs += [pl.BlockSpec((tm, BRANCH_W), rope_idx), pl.BlockSpec((tm, BRANCH_W), rope_idx)]
        args += [consts["rope_cos"], consts["rope_sin"]]

    def cx_idx(i):
        b = i // tiles_per_seq
        return (b // SUBLANES, i % tiles_per_seq, b % SUBLANES)

    def vt_spec(heads):
        return pl.BlockSpec((None, heads * VT_ROWS, tm), lambda i: (i // tiles_per_seq, 0, i % tiles_per_seq))

    def vt_shape(heads):
        return jax.ShapeDtypeStruct((batch, heads * VT_ROWS, seq), BF16)

    out_specs = [
        pl.BlockSpec((tm, BRANCH_W), row),
        pl.BlockSpec((tm, BRANCH_W), row),
        pl.BlockSpec((tm, BRANCH_W), row),
        vt_spec(NA_HEADS) if latent else pl.BlockSpec((tm, BRANCH_W), row),
        pl.BlockSpec((None, tm, BRANCH_W), cx_idx),
        pl.BlockSpec((tm, BRANCH_W), row),
        pl.BlockSpec((tm, 128), row),
        vt_spec(GQA_KV) if latent else pl.BlockSpec((tm, 128), row),
    ]
    out_shape = [
        jax.ShapeDtypeStruct((n, BRANCH_W), F32),
        jax.ShapeDtypeStruct((n, BRANCH_W), BF16),
        jax.ShapeDtypeStruct((n, BRANCH_W), kv_dtype),
        vt_shape(NA_HEADS) if latent else jax.ShapeDtypeStruct((n, BRANCH_W), F32),
        jax.ShapeDtypeStruct((halves, seq, SUBLANES * BRANCH_W), F32),
        jax.ShapeDtypeStruct((n, BRANCH_W), BF16),
        jax.ShapeDtypeStruct((n, 128), kv_dtype),
        vt_shape(GQA_KV) if latent else jax.ShapeDtypeStruct((n, 128), F32),
    ]
    return pl.pallas_call(
        functools.partial(_inproj_kernel, tm=tm, latent=latent),
        grid=(n // tm,),
        in_specs=in_specs,
        out_specs=out_specs,
        out_shape=out_shape,
        compiler_params=_params("parallel"),
        name="inproj_latent" if latent else "inproj_ctx",
    )(*args)


def _block_attn_kernel(q_ref, k_ref, vt_ref, *rest, tq, kvh, rep, lk, ck, cache_len):
    if cache_len:
        ck_ref, cvt_ref, o_ref, s_scr = rest
    else:
        o_ref, s_scr = rest
    steps = []
    for j in range(kvh):
        steps += [(j, c, False) for c in range(lk // ck)]
        if cache_len:
            steps.append((j, 0, True))
    per_head = len(steps) // kvh
    qs = [jnp.concatenate([q_ref[:, (j * rep + g) * HEAD_DIM:(j * rep + g + 1) * HEAD_DIM] for g in range(rep)],
                          axis=0) for j in range(kvh)]

    def logits(t):
        j, c, cached = steps[t]
        lanes = slice(j * HEAD_DIM, (j + 1) * HEAD_DIM)
        kc = ck_ref[:, lanes] if cached else k_ref[c * ck:(c + 1) * ck, lanes]
        st = _dot_nt(kc.astype(BF16), qs[j])
        s_scr[t % 2, 0:st.shape[0], :] = st
        return jnp.max(st, axis=0, keepdims=True)

    cmax = logits(0)
    m = acc = None
    heads_t = []
    for t, (j, c, cached) in enumerate(steps):
        nxt = logits(t + 1) if t + 1 < len(steps) else None
        n_keys = cache_len if cached else ck
        vrows = slice(j * VT_ROWS, (j + 1) * VT_ROWS)
        vt = cvt_ref[vrows, :] if cached else vt_ref[vrows, c * ck:(c + 1) * ck]
        st = s_scr[t % 2, 0:n_keys, :]
        if t % per_head == 0:
            m = cmax
            acc = _dot(vt, jnp.exp2(st - m).astype(BF16))
        else:
            m_new = jnp.maximum(m, cmax)
            acc = jnp.exp2(m - m_new) * acc + _dot(vt, jnp.exp2(st - m_new).astype(BF16))
            m = m_new
        if t % per_head == per_head - 1:
            o = acc[0:HEAD_DIM] / acc[HEAD_DIM:HEAD_DIM + 1]
            heads_t += [o[:, g * tq:(g + 1) * tq] for g in range(rep)]
        cmax = nxt
    o_ref[...] = jnp.concatenate(heads_t, axis=0).T


def _vt_layout(v, batch, seq, kvh):
    vt = v.reshape(batch, seq, kvh, HEAD_DIM).transpose(0, 2, 3, 1).astype(BF16)
    ones = jnp.ones((batch, kvh, VT_ROWS - HEAD_DIM, seq), BF16)
    return jnp.concatenate([vt, ones], axis=2).reshape(batch, kvh * VT_ROWS, seq)


def _block_attention(q, k, vt, cache_k, cache_vt, *, batch, seq, kvh, rep, tq, ck, name):
    n = batch * seq
    wkv = kvh * HEAD_DIM
    nq = seq // tq
    cache_len = 0 if cache_k is None else cache_k.shape[1]
    in_specs = [
        pl.BlockSpec((tq, BRANCH_W), lambda b, i: (b * nq + i, 0)),
        pl.BlockSpec((None, seq, wkv), lambda b, i: (b, 0, 0)),
        pl.BlockSpec((None, kvh * VT_ROWS, seq), lambda b, i: (b, 0, 0)),
    ]
    args = [q, k.reshape(batch, seq, wkv), vt]
    if cache_len:
        in_specs += [pl.BlockSpec((None, cache_len, wkv), lambda b, i: (b, 0, 0)),
                     pl.BlockSpec((None, kvh * VT_ROWS, cache_len), lambda b, i: (b, 0, 0))]
        args += [cache_k, cache_vt]
    return pl.pallas_call(
        functools.partial(_block_attn_kernel, tq=tq, kvh=kvh, rep=rep, lk=seq, ck=ck, cache_len=cache_len),
        grid=(batch, nq),
        in_specs=in_specs,
        out_specs=pl.BlockSpec((tq, BRANCH_W), lambda b, i: (b * nq + i, 0)),
        out_shape=jax.ShapeDtypeStruct((n, BRANCH_W), F32),
        scratch_shapes=[pltpu.VMEM((2, max(ck, cache_len), rep * tq), F32)],
        compiler_params=_params("parallel", "arbitrary"),
        name=name,
    )(*args)


NA_QROWS = 4
NA_KROWS = 12
NA_TQ = NA_QROWS * GRID_W
NA_TK = NA_KROWS * GRID_W
NA_BLOCKS_PER_STEP = 4


def _na_kernel(q_ref, k_ref, vt_ref, ck_ref, cvt_ref, bias_ref, o_ref, s_scr, *, rows, cache_len):
    i = pl.program_id(1)
    nq = rows // NA_QROWS
    r0s, kinds = [], []
    for u in range(NA_BLOCKS_PER_STEP):
        qi = i * NA_BLOCKS_PER_STEP + u
        kbase = jnp.clip(qi * NA_QROWS - NA_KH // 2, 0, rows - NA_KROWS)
        r0s.append(pl.multiple_of(kbase * GRID_W, 2 * GRID_W))
        kinds.append(jnp.where(qi == 0, 0, jnp.where(qi == nq - 1, 2, 1)))
    steps = [(u, h) for u in range(NA_BLOCKS_PER_STEP) for h in range(NA_HEADS)]

    def logits(t):
        u, h = steps[t]
        lanes = slice(h * HEAD_DIM, (h + 1) * HEAD_DIM)
        qh = q_ref[u * NA_TQ:(u + 1) * NA_TQ, lanes]
        s_win = _dot_nt(k_ref[pl.ds(r0s[u], NA_TK), lanes], qh) + bias_ref[kinds[u], h]
        s_ctx = _dot_nt(ck_ref[:, lanes], qh)
        s_scr[t % 2, 0:NA_TK, :] = s_win
        s_scr[t % 2, NA_TK:NA_TK + cache_len, :] = s_ctx
        return jnp.maximum(jnp.max(s_win, axis=0, keepdims=True), jnp.max(s_ctx, axis=0, keepdims=True))

    m = logits(0)
    heads_t = []
    for t, (u, h) in enumerate(steps):
        nxt = logits(t + 1) if t + 1 < len(steps) else None
        vrows = slice(h * VT_ROWS, (h + 1) * VT_ROWS)
        p_win = jnp.exp2(s_scr[t % 2, 0:NA_TK, :] - m).astype(BF16)
        p_ctx = jnp.exp2(s_scr[t % 2, NA_TK:NA_TK + cache_len, :] - m).astype(BF16)
        acc = _dot(vt_ref[vrows, pl.ds(r0s[u], NA_TK)], p_win) + _dot(cvt_ref[vrows, :], p_ctx)
        heads_t.append(acc[0:HEAD_DIM] / acc[HEAD_DIM:HEAD_DIM + 1])
        if h == NA_HEADS - 1:
            o_ref[u * NA_TQ:(u + 1) * NA_TQ, :] = jnp.concatenate(heads_t, axis=0).T
            heads_t = []
        m = nxt


def _na_attention(q, k, vt, cache_k, cache_vt, bias, *, batch, seq):
    n = batch * seq
    rows = seq // GRID_W
    nq = rows // NA_QROWS
    cache_len = cache_k.shape[1]
    vt_rows = NA_HEADS * VT_ROWS

    nstep = nq // NA_BLOCKS_PER_STEP
    tq = NA_BLOCKS_PER_STEP * NA_TQ
    return pl.pallas_call(
        functools.partial(_na_kernel, rows=rows, cache_len=cache_len),
        grid=(batch, nstep),
        in_specs=[
            pl.BlockSpec((tq, BRANCH_W), lambda b, i: (b * nstep + i, 0)),
            pl.BlockSpec((None, seq, BRANCH_W), lambda b, i: (b, 0, 0)),
            pl.BlockSpec((None, vt_rows, seq), lambda b, i: (b, 0, 0)),
            pl.BlockSpec((None, cache_len, BRANCH_W), lambda b, i: (b, 0, 0)),
            pl.BlockSpec((None, vt_rows, cache_len), lambda b, i: (b, 0, 0)),
            _full((3, NA_HEADS, NA_TK, NA_TQ)),
        ],
        out_specs=pl.BlockSpec((tq, BRANCH_W), lambda b, i: (b * nstep + i, 0)),
        out_shape=jax.ShapeDtypeStruct((n, BRANCH_W), F32),
        scratch_shapes=[pltpu.VMEM((2, NA_TK + cache_len, NA_TQ), F32)],
        compiler_params=_params("parallel", "arbitrary"),
        name="na_latent",
    )(q, k.reshape(batch, seq, BRANCH_W), vt, cache_k, cache_vt, bias)


def _na_bias_tables(rel_bias, rows):
    nq = rows // NA_QROWS
    kh = min(NA_KH, rows)
    n_dr, n_dc = 2 * NA_KH - 1, 2 * NA_KW - 1
    cq = np.arange(GRID_W)
    start_c = np.clip(cq - NA_KW // 2, 0, GRID_W - NA_KW)
    col_ok = (cq[None, :] >= start_c[:, None]) & (cq[None, :] < start_c[:, None] + NA_KW)
    dc = np.clip(cq[None, :] - cq[:, None], -(NA_KW - 1), NA_KW - 1) + (NA_KW - 1)
    onehot = (dc.T.reshape(1, -1) == np.arange(n_dc)[:, None]).astype(np.float32)
    per_dr = jnp.einsum("hrc,cx->hrx", rel_bias.astype(F32), jnp.asarray(onehot), precision=lax.Precision.HIGHEST)
    per_dr = jnp.where(col_ok.T.reshape(1, 1, -1), per_dr * math.log2(math.e), NEG_INF)
    masked = jnp.full((NA_HEADS, 1, GRID_W * GRID_W), NEG_INF, F32)
    per_dr = jnp.concatenate([per_dr, masked], axis=1).reshape(NA_HEADS, n_dr + 1, GRID_W, GRID_W)
    tabs = []
    for blk in (0, 1, nq - 1):
        kbase = int(np.clip(blk * NA_QROWS - NA_KH // 2, 0, rows - NA_KROWS))
        r = blk * NA_QROWS + np.arange(NA_QROWS)
        start_r = np.clip(r - kh // 2, 0, rows - kh)
        kr = kbase + np.arange(NA_KROWS)
        row_ok = (kr[None, :] >= start_r[:, None]) & (kr[None, :] < start_r[:, None] + kh)
        dr = np.where(row_ok, kr[None, :] - r[:, None] + (NA_KH - 1), n_dr)
        tabs.append(jnp.concatenate(
            [jnp.concatenate([per_dr[:, int(dr[a, b])] for a in range(NA_QROWS)], axis=-1)
             for b in range(NA_KROWS)], axis=-2))
    return jnp.stack(tabs, axis=0)


def _gelu_tanh(y):
    return 0.5 * y * (1.0 + jnp.tanh(math.sqrt(2.0 / math.pi) * (y + 0.044715 * (y * y * y))))


S5_PARTS = 1


def _aligned(offset, multiple):
    return offset if isinstance(offset, int) else pl.multiple_of(offset, multiple)


def _s5_kernel(x0_ref, xn_ref, bmat_ref, cmat_ref, abar_ref, s0_ref, *rest, steps, reverse, nchunk):
    if reverse:
        xprev_ref, yprev_ref, dvec_ref, wglu_ref, bglu_ref, y_ref, sfin_ref, bu0, bu1, sb0, sb1, st_ref = rest
    else:
        y_ref, sfin_ref, bu0, bu1, sb0, sb1, st_ref = rest
    j = pl.program_id(1)

    @pl.when(j == 0)
    def _():
        st_ref[...] = s0_ref[...]
        sb0[...] = jnp.zeros_like(sb0)
        bu0[...] = _dot(x0_ref[...].astype(BF16), bmat_ref[...])

    pair_rows = 2 * SUBLANES
    part_rows = steps * SUBLANES // S5_PARTS
    pairs = part_rows // pair_rows

    def stage(bu_cur, bu_nxt, sb_prev, sb_cur):
        a_re = jnp.broadcast_to(abar_ref[:, 0:SSM_N], (SUBLANES, SSM_N))
        a_im = jnp.broadcast_to(abar_ref[:, SSM_N:], (SUBLANES, SSM_N))
        old_re, old_im = st_ref[:, 0:SSM_N], st_ref[:, SSM_N:]

        def part(q, carry):
            s_re, s_im = carry
            mrows = pl.ds(_aligned(q * part_rows, part_rows), part_rows)
            y = _dot(sb_prev[mrows, :], cmat_ref[...])
            bu_nxt[mrows, :] = _dot(xn_ref[mrows, :].astype(BF16), bmat_ref[...])
            if reverse:
                y = yprev_ref[mrows, :] + y + dvec_ref[...] * xprev_ref[mrows, :]
                y = _gelu_tanh(y)
                y = y * jax.nn.sigmoid(_dot(y.astype(BF16), wglu_ref[...]) + bglu_ref[...])
            y_ref[mrows, :] = y

            base = ((S5_PARTS - 1 - q) if reverse else q) * part_rows
            for i in range(pairs):
                r0 = _aligned(base + ((pairs - 1 - i) if reverse else i) * pair_rows, pair_rows)
                out_re, out_im = [None, None], [None, None]
                for half in ((1, 0) if reverse else (0, 1)):
                    rr = _aligned(r0 + half * SUBLANES, SUBLANES)
                    n_re = a_re * s_re - a_im * s_im + bu_cur[pl.ds(rr, SUBLANES), 0:SSM_N]
                    n_im = a_re * s_im + a_im * s_re + bu_cur[pl.ds(rr, SUBLANES), SSM_N:]
                    out_re[half], out_im[half] = n_re, n_im
                    s_re, s_im = n_re, n_im
                sb_cur[pl.ds(r0, pair_rows), 0:SSM_N] = jnp.concatenate(out_re, axis=0).astype(BF16)
                sb_cur[pl.ds(r0, pair_rows), SSM_N:] = jnp.concatenate(out_im, axis=0).astype(BF16)
            return s_re, s_im

        if S5_PARTS == 1:
            s_re, s_im = part(0, (old_re, old_im))
        else:
            s_re, s_im = lax.fori_loop(0, S5_PARTS, part, (old_re, old_im))
        live = j < nchunk
        st_ref[:, 0:SSM_N] = jnp.where(live, s_re, old_re)
        st_ref[:, SSM_N:] = jnp.where(live, s_im, old_im)

    @pl.when(j % 2 == 0)
    def _():
        stage(bu0, bu1, sb0, sb1)

    @pl.when(j % 2 == 1)
    def _():
        stage(bu1, bu0, sb1, sb0)

    sfin_ref[...] = st_ref[...]


def _s5_pass(x3, sp, s0, yprev, lw, *, steps, reverse):
    halves, rows, _ = x3.shape
    nchunk = rows // (steps * SUBLANES)
    tr = steps * SUBLANES

    def chunk_of(step):
        step = jnp.clip(step, 0, nchunk - 1)
        return (nchunk - 1 - step) if reverse else step

    first = lambda h, j: (h, chunk_of(0), 0)
    nxt = lambda h, j: (h, chunk_of(j + 1), 0)
    prev = lambda h, j: (h, chunk_of(j - 1), 0)

    in_specs = [
        pl.BlockSpec((None, tr, BRANCH_W), first),
        pl.BlockSpec((None, tr, BRANCH_W), nxt),
        _full((BRANCH_W, 2 * SSM_N)),
        _full((2 * SSM_N, BRANCH_W)),
        _full((1, 2 * SSM_N)),
        pl.BlockSpec((None, SUBLANES, 2 * SSM_N), lambda h, j: (h, 0, 0)),
    ]
    args = [x3, x3, sp["bmat"], sp["cmat"], sp["abar"], s0]
    if reverse:
        in_specs += [pl.BlockSpec((None, tr, BRANCH_W), prev), pl.BlockSpec((None, tr, BRANCH_W), prev),
                     _full((1, BRANCH_W)), _full((BRANCH_W, BRANCH_W)), _full((1, BRANCH_W))]
        args += [x3, yprev, lw["ssm_d"], lw["w_glu"], lw["b_glu"]]
    return pl.pallas_call(
        functools.partial(_s5_kernel, steps=steps, reverse=reverse, nchunk=nchunk),
        grid=(halves, nchunk + 1),
        in_specs=in_specs,
        out_specs=[pl.BlockSpec((None, tr, BRANCH_W), prev),
                   pl.BlockSpec((None, SUBLANES, 2 * SSM_N), lambda h, j: (h, 0, 0))],
        out_shape=[jax.ShapeDtypeStruct(x3.shape, F32),
                   jax.ShapeDtypeStruct((halves, SUBLANES, 2 * SSM_N), F32)],
        scratch_shapes=[pltpu.VMEM((tr, 2 * SSM_N), F32), pltpu.VMEM((tr, 2 * SSM_N), F32),
                        pltpu.VMEM((tr, 2 * SSM_N), BF16), pltpu.VMEM((tr, 2 * SSM_N), BF16),
                        pltpu.VMEM((SUBLANES, 2 * SSM_N), F32)],
        compiler_params=_params("parallel", "arbitrary"),
        name="s5_reverse" if reverse else "s5_forward",
    )(*args)


def _s5_discretise(a_re, a_im, log_dt, b_re, b_im, c_re, c_im):
    dt = jnp.exp(log_dt)[:, None]
    mag = jnp.exp(dt * a_re)
    ab_re, ab_im = mag * jnp.cos(dt * a_im), mag * jnp.sin(dt * a_im)
    den = a_re * a_re + a_im * a_im
    nr, ni = ab_re - 1.0, ab_im
    coef_re = ((nr * a_re + ni * a_im) / den)[..., None]
    coef_im = ((ni * a_re - nr * a_im) / den)[..., None]
    bb_re = coef_re * b_re - coef_im * b_im
    bb_im = coef_re * b_im + coef_im * b_re
    eye = jnp.eye(SSM_G, dtype=F32)

    def in_blocks(bb):
        return jnp.einsum("gpi,gk->gikp", bb, eye).reshape(SSM_G * SSM_H, SSM_N)

    def out_blocks(cc):
        return jnp.einsum("gip,gk->gpki", cc, eye).reshape(SSM_N, SSM_G * SSM_H)

    bmat = jnp.concatenate([in_blocks(bb_re), in_blocks(bb_im)], axis=1).astype(BF16)
    cmat = jnp.concatenate([out_blocks(c_re), -out_blocks(c_im)], axis=0).astype(BF16)
    abar = jnp.concatenate([ab_re.reshape(1, SSM_N), ab_im.reshape(1, SSM_N)], axis=1)
    return dict(bmat=bmat, cmat=cmat, abar=abar)


def _state_to_rows(s):
    b = s.shape[0]
    flat = jnp.concatenate([s[..., 0].reshape(b, SSM_N), s[..., 1].reshape(b, SSM_N)], axis=1)
    return flat.reshape(b // SUBLANES, SUBLANES, 2 * SSM_N)


def _rows_to_state(r):
    b = r.shape[0] * SUBLANES
    flat = r.reshape(b, 2, SSM_G, SSM_P)
    return jnp.stack([flat[:, 0], flat[:, 1]], axis=-1)


MERGE_SUB_ROWS = 256


def _merge_kernel(x_ref, g_ref, scale_ref, shift_ref, gate_ref, ya_ref, yb_ref, yc_ref, yd_ref,
                  wz_ref, wg_ref, wbr_ref, wout_ref, fg_ref, o_ref, *, last):
    tm = x_ref.shape[0]
    sub = min(tm, MERGE_SUB_ROWS)
    for r0 in range(0, tm, sub):
        rows = slice(r0, r0 + sub)
        x = x_ref[rows, :]
        hb = _modulated_norm(x, g_ref[...], scale_ref[...], shift_ref[...]).astype(BF16)
        z = _dot(hb, wz_ref[...])
        merged = None
        for n, y_ref in enumerate((ya_ref, yb_ref, yc_ref, yd_ref)):
            zn = z[:, n * BRANCH_W:(n + 1) * BRANCH_W]
            yn = y_ref[rows, :] * (zn * jax.nn.sigmoid(zn))
            t = _dot(yn.astype(BF16), wbr_ref[n * BRANCH_W:(n + 1) * BRANCH_W, :])
            gn = _dot(hb, wg_ref[:, n * D_MODEL:(n + 1) * D_MODEL])
            term = jax.nn.sigmoid(gn) * t
            merged = term if merged is None else merged + term
        out = x + gate_ref[...] * _dot(merged.astype(BF16), wout_ref[...])
        if last:
            out = (out * lax.rsqrt(jnp.mean(out * out, axis=-1, keepdims=True) + EPS)) * fg_ref[...]
        o_ref[rows, :] = out


def _merge(x, mod_scale, mod_shift, mod_gate, ya, yb, yc3, yd, lw, final_g, *, batch, seq, tm, last, name):
    n = batch * seq
    tiles_per_seq = seq // tm
    per_batch_mod = mod_scale.shape[0] > 1

    def mod_idx(i):
        return ((i // tiles_per_seq) if per_batch_mod else 0, 0, 0)

    def yc_idx(i):
        b = i // tiles_per_seq
        return (b // SUBLANES, i % tiles_per_seq, b % SUBLANES)

    row = lambda i: (i, 0)
    ytile = pl.BlockSpec((tm, BRANCH_W), row)
    return pl.pallas_call(
        functools.partial(_merge_kernel, last=last),
        grid=(n // tm,),
        in_specs=[
            pl.BlockSpec((tm, D_MODEL), row),
            _full((1, D_MODEL)),
            pl.BlockSpec((None, 1, D_MODEL), mod_idx),
            pl.BlockSpec((None, 1, D_MODEL), mod_idx),
            pl.BlockSpec((None, 1, D_MODEL), mod_idx),
            ytile, ytile,
            pl.BlockSpec((None, tm, BRANCH_W), yc_idx),
            ytile,
            _full((D_MODEL, 4 * BRANCH_W)),
            _full((D_MODEL, 4 * D_MODEL)),
            _full((4 * BRANCH_W, D_MODEL)),
            _full((D_MODEL, D_MODEL)),
            _full((1, D_MODEL)),
        ],
        out_specs=pl.BlockSpec((tm, D_MODEL), row),
        out_shape=jax.ShapeDtypeStruct((n, D_MODEL), F32),
        compiler_params=_params("parallel"),
        name=name,
    )(x, lw["norm_g"], mod_scale, mod_shift, mod_gate, ya, yb, yc3, yd,
      lw["w_z"], lw["w_g"], lw["w_branch"], lw["w_out"], final_g)


def _rope_tables(seq):
    t = np.arange(seq)
    row = (t // GRID_W).astype(np.float32)
    col = (t % GRID_W).astype(np.float32)
    nf = HEAD_DIM // 4
    freqs = jnp.asarray(ROPE_BASE, F32) ** (-jnp.arange(nf, dtype=F32) / nf)
    ang_r = jnp.asarray(row)[:, None] * freqs[None, :]
    ang_c = jnp.asarray(col)[:, None] * freqs[None, :]
    cos = jnp.concatenate([jnp.cos(ang_r)] * 2 + [jnp.cos(ang_c)] * 2, axis=1)
    sin = jnp.concatenate([-jnp.sin(ang_r), jnp.sin(ang_r), -jnp.sin(ang_c), jnp.sin(ang_c)], axis=1)
    return jnp.tile(cos, (1, GQA_HEADS)), jnp.tile(sin, (1, GQA_HEADS))


def _layer_weights(l, norm_g, w_in, gm_v_g, gm_ws, gm_b, ssm_d, w_glu, b_glu, gqa_q_g, gqa_k_g, w_branch, w_out):
    w = w_in[l]
    w_proj = jnp.concatenate([w[:, 0:512], w[:, 768:1536], w[:, 1792:2048], w[:, 2304:2816]], axis=1).astype(BF16)
    w_z = jnp.concatenate([w[:, 512:768], w[:, 1536:1792], w[:, 2048:2304], w[:, 2816:3072]], axis=1).astype(BF16)
    return dict(
        norm_g=norm_g[l].reshape(1, D_MODEL),
        w_proj=w_proj,
        w_z=w_z,
        w_g=w[:, 3072:].astype(BF16),
        gm_v_g=gm_v_g[l].reshape(1, BRANCH_W),
        gm_w=gm_ws[l].reshape(GM_GROUPS * CHUNK, CHUNK).astype(BF16),
        gm_bias=jnp.repeat(gm_b[l].T, HEAD_DIM, axis=1),
        ssm_d=ssm_d[l].reshape(1, BRANCH_W),
        w_glu=w_glu[l].astype(BF16),
        b_glu=b_glu[l].reshape(1, BRANCH_W),
        gqa_q_g=jnp.tile(gqa_q_g[l], GQA_HEADS).reshape(1, BRANCH_W),
        gqa_k_g=jnp.tile(gqa_k_g[l], GQA_KV).reshape(1, 128),
        w_branch=w_branch[l].reshape(4 * BRANCH_W, D_MODEL).astype(BF16),
        w_out=w_out[l].astype(BF16),
    )


def _trunk_layer(x, mods, lw, sps, consts, final_g, cache, *, batch, seq, latent, last):
    scale, shift, gate = mods
    tm = 1024 if latent else 256
    ya, bq, bk, bv, cx, dq, dk, dv = _inproj(x, scale, shift, lw, consts, batch=batch, seq=seq, tm=tm, latent=latent)
    halves = batch // SUBLANES
    cx3 = cx.reshape(halves, seq * SUBLANES, BRANCH_W)
    if latent:
        yb = _na_attention(bq, bk, bv, cache["na_k"], cache["na_vt"], cache["na_bias"], batch=batch, seq=seq)
        yd = _block_attention(dq, dk, dv, cache["gqa_k"], cache["gqa_vt"], batch=batch, seq=seq,
                              kvh=GQA_KV, rep=GQA_HEADS // GQA_KV, tq=256, ck=512, name="gqa_latent")
        s0 = cache["ssm"]
    else:
        yb = _block_attention(bq, bk, _vt_layout(bv, batch, seq, NA_HEADS), None, None, batch=batch, seq=seq,
                              kvh=NA_HEADS, rep=1, tq=seq, ck=seq, name="na_ctx")
        yd = _block_attention(dq, dk, _vt_layout(dv, batch, seq, GQA_KV), None, None, batch=batch, seq=seq,
                              kvh=GQA_KV, rep=GQA_HEADS // GQA_KV, tq=seq, ck=seq, name="gqa_ctx")
        zero = jnp.zeros((halves, SUBLANES, 2 * SSM_N), F32)
        s0 = (zero, zero)
    steps = 128 if latent else 64
    yf, sf = _s5_pass(cx3, sps[0], s0[0], None, lw, steps=steps, reverse=False)
    yc3, sr = _s5_pass(cx3, sps[1], s0[1], yf, lw, steps=steps, reverse=True)
    yc3 = yc3.reshape(halves, seq, SUBLANES * BRANCH_W)
    x_new = _merge(x, scale, shift, gate, ya, yb, yc3, yd, lw, final_g, batch=batch, seq=seq, tm=tm, last=last,
                   name="merge_latent" if latent else "merge_ctx")
    return x_new, (bk, bv, dk, dv, sf, sr)


def kernel(x_prompt, x_sample, c, cache_na_k, cache_na_v, cache_gqa_k, cache_gqa_v, state_ssm, c_ctx,
           norm_g, w_ada, b_ada, w_in, gm_v_g, gm_ws, gm_b, na_rel_bias, ssm_a_re, ssm_a_im, ssm_log_dt,
           ssm_b_re, ssm_b_im, ssm_c_re, ssm_c_im, ssm_d, w_glu, b_glu, gqa_q_g, gqa_k_g, w_branch, w_out,
           final_g):
    bc, lc, _ = x_prompt.shape
    bl, ll, _ = x_sample.shape
    past = cache_na_k.shape[2]

    n_rows = 16
    cond = jnp.zeros((n_rows, D_MODEL), F32).at[0].set(c_ctx).at[1:1 + bl].set(c)
    mod = _modulation(cond, w_ada, b_ada)

    def mods_of(l, lo, hi):
        m = mod[l, lo:hi].reshape(hi - lo, 1, 3 * D_MODEL)
        return m[..., 0:D_MODEL], m[..., D_MODEL:2 * D_MODEL], m[..., 2 * D_MODEL:]

    cos, sin = _rope_tables(ll)
    ones_blk = jnp.asarray(np.kron(np.eye(BRANCH_W // HEAD_DIM), np.ones((HEAD_DIM, HEAD_DIM))), BF16)
    consts = dict(rope_cos=cos, rope_sin=sin, ones_blk=ones_blk)
    fg = final_g.reshape(1, D_MODEL)

    xp = x_prompt.reshape(bc * lc, D_MODEL)
    xs = x_sample.reshape(bl * ll, D_MODEL)
    na_k_l, na_v_l, gqa_k_l, gqa_v_l, ssm_l = [], [], [], [], []
    for l in range(DEPTH):
        lw = _layer_weights(l, norm_g, w_in, gm_v_g, gm_ws, gm_b, ssm_d, w_glu, b_glu, gqa_q_g, gqa_k_g,
                            w_branch, w_out)
        sps = [_s5_discretise(ssm_a_re[l, d], ssm_a_im[l, d], ssm_log_dt[l, d], ssm_b_re[l, d], ssm_b_im[l, d],
                              ssm_c_re[l, d], ssm_c_im[l, d]) for d in range(2)]
        last = l == DEPTH - 1
        shift, scale, gate = mods_of(l, 0, 1)
        xp, (k_na, v_na, k_g, v_g, sf, sr) = _trunk_layer(
            xp, (scale, shift, gate), lw, sps, consts, fg, None, batch=bc, seq=lc, latent=False, last=last)
        na_k_l.append(k_na.reshape(bc, lc, NA_HEADS, HEAD_DIM))
        na_v_l.append(v_na.reshape(bc, lc, NA_HEADS, HEAD_DIM))
        gqa_k_l.append(k_g.reshape(bc, lc, GQA_KV, HEAD_DIM))
        gqa_v_l.append(v_g.reshape(bc, lc, GQA_KV, HEAD_DIM))
        ssm_l.append(jnp.stack([_rows_to_state(sf), _rows_to_state(sr)], axis=1))
        shift, scale, gate = mods_of(l, 1, 1 + bl)
        cache = dict(
            na_k=cache_na_k[:, l].reshape(bl, past, BRANCH_W).astype(BF16),
            na_vt=_vt_layout(cache_na_v[:, l], bl, past, NA_HEADS),
            gqa_k=cache_gqa_k[:, l].reshape(bl, past, GQA_KV * HEAD_DIM).astype(BF16),
            gqa_vt=_vt_layout(cache_gqa_v[:, l], bl, past, GQA_KV),
            ssm=(_state_to_rows(state_ssm[:, l, 0]), _state_to_rows(state_ssm[:, l, 1])),
            na_bias=_na_bias_tables(na_rel_bias[l], ll // GRID_W),
        )
        xs, _ = _trunk_layer(xs, (scale, shift, gate), lw, sps, consts, fg, cache,
                             batch=bl, seq=ll, latent=True, last=last)
    return (xp.reshape(bc, lc, D_MODEL), xs.reshape(bl, ll, D_MODEL),
            jnp.stack(na_k_l, axis=1), jnp.stack(na_v_l, axis=1),
            jnp.stack(gqa_k_l, axis=1), jnp.stack(gqa_v_l, axis=1), jnp.stack(ssm_l, axis=1))
```

```python
import functools
import math

import numpy as np
import jax
import jax.numpy as jnp
from jax import lax
from jax.experimental import pallas as pl
from jax.experimental.pallas import tpu as pltpu

D_MODEL = 1024
DEPTH = 2
GRID_W = 64
BRANCH_W = 256
HEAD_DIM = 64
CHUNK = 128
GM_GROUPS = 4
NA_HEADS = 4
NA_KH = 8
NA_KW = 16
SSM_H = 16
SSM_G = 16
SSM_P = 64
SSM_N = SSM_G * SSM_P
GQA_HEADS = 4
GQA_KV = 2
ROPE_BASE = 10000.0
EPS = 1e-6
NEG_INF = -1e30
ATTN_SCALE = HEAD_DIM ** -0.5

SUBLANES = 8
VMEM_LIMIT = 56 * 1024 * 1024

F32 = jnp.float32
BF16 = jnp.bfloat16


def _params(*sem):
    return pltpu.CompilerParams(dimension_semantics=sem, vmem_limit_bytes=VMEM_LIMIT)


def _dot(a, b):
    return jnp.dot(a, b, preferred_element_type=F32)


def _dot_nt(a, b):
    return lax.dot_general(a, b, (((1,), (1,)), ((), ())), preferred_element_type=F32)


def _full(shape):
    nd = len(shape)
    return pl.BlockSpec(shape, lambda *_: (0,) * nd)


def _modulated_norm(x, g, scale, shift):
    y = x * lax.rsqrt(jnp.mean(x * x, axis=-1, keepdims=True) + EPS)
    return (y * g) * (1.0 + scale) + shift


def _group_sumsq(x, ones_blk):
    x2 = x * x
    hi = x2.astype(BF16)
    lo = (x2 - hi.astype(F32)).astype(BF16)
    return _dot(hi, ones_blk) + _dot(lo, ones_blk)


def _head_rmsnorm(x, ones_blk, g):
    ss = _group_sumsq(x, ones_blk)
    return (x * lax.rsqrt(ss * (1.0 / HEAD_DIM) + EPS)) * g


def _rope(x, cos, sin_signed):
    w = x.shape[1]
    lane = lax.broadcasted_iota(jnp.int32, x.shape, 1)
    up = pltpu.roll(x, 16, axis=1)
    dn = pltpu.roll(x, w - 16, axis=1)
    partner = jnp.where((lane & 16) != 0, up, dn)
    return x * cos + partner * sin_signed


VT_ROWS = HEAD_DIM + 16
QK_SCALE_LOG2 = ATTN_SCALE * math.log2(math.e)


def _vt_tile(v, heads):
    vt = v.T.astype(BF16)
    ones = jnp.ones((VT_ROWS - HEAD_DIM, v.shape[0]), BF16)
    parts = []
    for h in range(heads):
        parts += [vt[h * HEAD_DIM:(h + 1) * HEAD_DIM], ones]
    return jnp.concatenate(parts, axis=0)


def _mod_kernel(cond_ref, w_ref, b_ref, o_ref):
    c = cond_ref[...]
    s = (c * jax.nn.sigmoid(c)).astype(BF16)
    o_ref[...] = _dot(s, w_ref[...].astype(BF16)) + b_ref[...]


def _modulation(cond, w_ada, b_ada):
    r = cond.shape[0]
    nj = 3
    return pl.pallas_call(
        _mod_kernel,
        grid=(DEPTH, nj),
        in_specs=[
            pl.BlockSpec((r, D_MODEL), lambda l, j: (0, 0)),
            pl.BlockSpec((None, D_MODEL, D_MODEL), lambda l, j: (l, 0, j)),
            pl.BlockSpec((None, 1, D_MODEL), lambda l, j: (l, 0, j)),
        ],
        out_specs=pl.BlockSpec((None, r, D_MODEL), lambda l, j: (l, 0, j)),
        out_shape=jax.ShapeDtypeStruct((DEPTH, r, 3 * D_MODEL), F32),
        compiler_params=_params("arbitrary", "arbitrary"),
        name="adaln_modulation",
    )(cond, w_ada, b_ada.reshape(DEPTH, 1, 3 * D_MODEL))


INPROJ_SUB_ROWS = 256


def _inproj_kernel(x_ref, g_ref, scale_ref, shift_ref, w_ref, gmg_ref, gmw_ref, gmb_ref, ones_ref,
                   qg_ref, kg_ref, *rest, tm, latent):
    if latent:
        cos_ref, sin_ref = rest[:2]
        rest = rest[2:]
    ya_ref, bq_ref, bk_ref, bv_ref, cx_ref, dq_ref, dk_ref, dv_ref = rest

    lane_grp = lax.broadcasted_iota(jnp.int32, (CHUNK, BRANCH_W), 1) // HEAD_DIM
    ones_blk = ones_ref[...]
    sub = min(tm, INPROJ_SUB_ROWS)
    for r0 in range(0, tm, sub):
        rows = slice(r0, r0 + sub)
        h = _modulated_norm(x_ref[rows, :], g_ref[...], scale_ref[...], shift_ref[...])
        p = _dot(h.astype(BF16), w_ref[...])

        u = p[:, 0:256]
        v = p[:, 256:512]
        vn = (v * lax.rsqrt(jnp.mean(v * v, axis=-1, keepdims=True) + EPS)) * gmg_ref[...]
        vnb = vn.astype(BF16)
        for c in range(sub // CHUNK):
            r = _dot(gmw_ref[...], vnb[c * CHUNK:(c + 1) * CHUNK, :])
            sp = r[0:CHUNK]
            for g in range(1, GM_GROUPS):
                sp = jnp.where(lane_grp == g, r[g * CHUNK:(g + 1) * CHUNK], sp)
            sp = sp + gmb_ref[...]
            ya_ref[r0 + c * CHUNK:r0 + (c + 1) * CHUNK, :] = u[c * CHUNK:(c + 1) * CHUNK, :] * sp

        bq_ref[rows, :] = (p[:, 512:768] * QK_SCALE_LOG2).astype(bq_ref.dtype)
        bk_ref[rows, :] = p[:, 768:1024].astype(bk_ref.dtype)
        if latent:
            bv_ref[:, rows] = _vt_tile(p[:, 1024:1280], NA_HEADS)
        else:
            bv_ref[rows, :] = p[:, 1024:1280]
        cx_ref[rows, :] = p[:, 1280:1536]
        dq = _head_rmsnorm(p[:, 1536:1792], ones_blk, qg_ref[...])
        dk = _head_rmsnorm(p[:, 1792:1920], ones_blk[0:128, 0:128], kg_ref[...])
        if latent:
            cos = cos_ref[rows, :]
            sin = sin_ref[rows, :]
            dq = _rope(dq, cos, sin)
            dk = _rope(dk, cos[:, 0:128], sin[:, 0:128])
        dq_ref[rows, :] = (dq * QK_SCALE_LOG2).astype(dq_ref.dtype)
        dk_ref[rows, :] = dk.astype(dk_ref.dtype)
        if latent:
            dv_ref[:, rows] = _vt_tile(p[:, 1920:2048], GQA_KV)
        else:
            dv_ref[rows, :] = p[:, 1920:2048]


def _inproj(x, mod_scale, mod_shift, lw, consts, *, batch, seq, tm, latent):
    n = batch * seq
    tiles_per_seq = seq // tm
    halves = batch // SUBLANES
    per_batch_mod = mod_scale.shape[0] > 1

    def mod_idx(i):
        return ((i // tiles_per_seq) if per_batch_mod else 0, 0, 0)

    row = lambda i: (i, 0)
    kv_dtype = BF16 if latent else F32
    in_specs = [
        pl.BlockSpec((tm, D_MODEL), row),
        _full((1, D_MODEL)),
        pl.BlockSpec((None, 1, D_MODEL), mod_idx),
        pl.BlockSpec((None, 1, D_MODEL), mod_idx),
        _full((D_MODEL, 2048)),
        _full((1, BRANCH_W)),
        _full((GM_GROUPS * CHUNK, CHUNK)),
        _full((CHUNK, BRANCH_W)),
        _full((BRANCH_W, BRANCH_W)),
        _full((1, BRANCH_W)),
        _full((1, 128)),
    ]
    args = [x, lw["norm_g"], mod_scale, mod_shift, lw["w_proj"], lw["gm_v_g"], lw["gm_w"], lw["gm_bias"],
            consts["ones_blk"], lw["gqa_q_g"], lw["gqa_k_g"]]
    if latent:
        rope_idx = lambda i: (i % tiles_per_seq, 0)
        in_specs += [pl.BlockSpec((tm, BRANCH_W), rope_idx), pl.BlockSpec((tm, BRANCH_W), rope_idx)]
        args += [consts["rope_cos"], consts["rope_sin"]]

    def cx_idx(i):
        b = i // tiles_per_seq
        return (b // SUBLANES, i % tiles_per_seq, b % SUBLANES)

    def vt_spec(heads):
        return pl.BlockSpec((None, heads * VT_ROWS, tm), lambda i: (i // tiles_per_seq, 0, i % tiles_per_seq))

    def vt_shape(heads):
        return jax.ShapeDtypeStruct((batch, heads * VT_ROWS, seq), BF16)

    out_specs = [
        pl.BlockSpec((tm, BRANCH_W), row),
        pl.BlockSpec((tm, BRANCH_W), row),
        pl.BlockSpec((tm, BRANCH_W), row),
        vt_spec(NA_HEADS) if latent else pl.BlockSpec((tm, BRANCH_W), row),
        pl.BlockSpec((None, tm, BRANCH_W), cx_idx),
        pl.BlockSpec((tm, BRANCH_W), row),
        pl.BlockSpec((tm, 128), row),
        vt_spec(GQA_KV) if latent else pl.BlockSpec((tm, 128), row),
    ]
    out_shape = [
        jax.ShapeDtypeStruct((n, BRANCH_W), F32),
        jax.ShapeDtypeStruct((n, BRANCH_W), BF16),
        jax.ShapeDtypeStruct((n, BRANCH_W), kv_dtype),
        vt_shape(NA_HEADS) if latent else jax.ShapeDtypeStruct((n, BRANCH_W), F32),
        jax.ShapeDtypeStruct((halves, seq, SUBLANES * BRANCH_W), F32),
        jax.ShapeDtypeStruct((n, BRANCH_W), BF16),
        jax.ShapeDtypeStruct((n, 128), kv_dtype),
        vt_shape(GQA_KV) if latent else jax.ShapeDtypeStruct((n, 128), F32),
    ]
    return pl.pallas_call(
        functools.partial(_inproj_kernel, tm=tm, latent=latent),
        grid=(n // tm,),
        in_specs=in_specs,
        out_specs=out_specs,
        out_shape=out_shape,
        compiler_params=_params("parallel"),
        name="inproj_latent" if latent else "inproj_ctx",
    )(*args)


def _block_attn_kernel(q_ref, k_ref, vt_ref, *rest, tq, kvh, rep, lk, ck, cache_len):
    if cache_len:
        ck_ref, cvt_ref, o_ref, s_scr = rest
    else:
        o_ref, s_scr = rest
    steps = []
    for j in range(kvh):
        steps += [(j, c, False) for c in range(lk // ck)]
        if cache_len:
            steps.append((j, 0, True))
    per_head = len(steps) // kvh
    qs = [jnp.concatenate([q_ref[:, (j * rep + g) * HEAD_DIM:(j * rep + g + 1) * HEAD_DIM] for g in range(rep)],
                          axis=0) for j in range(kvh)]

    def logits(t):
        j, c, cached = steps[t]
        lanes = slice(j * HEAD_DIM, (j + 1) * HEAD_DIM)
        kc = ck_ref[:, lanes] if cached else k_ref[c * ck:(c + 1) * ck, lanes]
        st = _dot_nt(kc.astype(BF16), qs[j])
        s_scr[t % 2, 0:st.shape[0], :] = st
        return jnp.max(st, axis=0, keepdims=True)

    cmax = logits(0)
    m = acc = None
    heads_t = []
    for t, (j, c, cached) in enumerate(steps):
        nxt = logits(t + 1) if t + 1 < len(steps) else None
        n_keys = cache_len if cached else ck
        vrows = slice(j * VT_ROWS, (j + 1) * VT_ROWS)
        vt = cvt_ref[vrows, :] if cached else vt_ref[vrows, c * ck:(c + 1) * ck]
        st = s_scr[t % 2, 0:n_keys, :]
        if t % per_head == 0:
            m = cmax
            acc = _dot(vt, jnp.exp2(st - m).astype(BF16))
        else:
            m_new = jnp.maximum(m, cmax)
            acc = jnp.exp2(m - m_new) * acc + _dot(vt, jnp.exp2(st - m_new).astype(BF16))
            m = m_new
        if t % per_head == per_head - 1:
            o = acc[0:HEAD_DIM] / acc[HEAD_DIM:HEAD_DIM + 1]
            heads_t += [o[:, g * tq:(g + 1) * tq] for g in range(rep)]
        cmax = nxt
    o_ref[...] = jnp.concatenate(heads_t, axis=0).T


def _vt_layout(v, batch, seq, kvh):
    vt = v.reshape(batch, seq, kvh, HEAD_DIM).transpose(0, 2, 3, 1).astype(BF16)
    ones = jnp.ones((batch, kvh, VT_ROWS - HEAD_DIM, seq), BF16)
    return jnp.concatenate([vt, ones], axis=2).reshape(batch, kvh * VT_ROWS, seq)


def _cache_vt_layout(v):
    b, _, past, heads, _ = v.shape
    vt = v.transpose(1, 0, 3, 4, 2).astype(BF16)
    ones = jnp.ones((DEPTH, b, heads, VT_ROWS - HEAD_DIM, past), BF16)
    return jnp.concatenate([vt, ones], axis=3).reshape(DEPTH, b, heads * VT_ROWS, past)


def _block_attention(q, k, vt, cache_k, cache_vt, *, batch, seq, kvh, rep, tq, ck, name):
    n = batch * seq
    wkv = kvh * HEAD_DIM
    nq = seq // tq
    cache_len = 0 if cache_k is None else cache_k.shape[1]
    in_specs = [
        pl.BlockSpec((tq, BRANCH_W), lambda b, i: (b * nq + i, 0)),
        pl.BlockSpec((None, seq, wkv), lambda b, i: (b, 0, 0)),
        pl.BlockSpec((None, kvh * VT_ROWS, seq), lambda b, i: (b, 0, 0)),
    ]
    args = [q, k.reshape(batch, seq, wkv), vt]
    if cache_len:
        in_specs += [pl.BlockSpec((None, cache_len, wkv), lambda b, i: (b, 0, 0)),
                     pl.BlockSpec((None, kvh * VT_ROWS, cache_len), lambda b, i: (b, 0, 0))]
        args += [cache_k, cache_vt]
    return pl.pallas_call(
        functools.partial(_block_attn_kernel, tq=tq, kvh=kvh, rep=rep, lk=seq, ck=ck, cache_len=cache_len),
        grid=(batch, nq),
        in_specs=in_specs,
        out_specs=pl.BlockSpec((tq, BRANCH_W), lambda b, i: (b * nq + i, 0)),
        out_shape=jax.ShapeDtypeStruct((n, BRANCH_W), F32),
        scratch_shapes=[pltpu.VMEM((2, max(ck, cache_len), rep * tq), F32)],
        compiler_params=_params("parallel", "arbitrary"),
        name=name,
    )(*args)


NA_QROWS = 4
NA_KROWS = 12
NA_TQ = NA_QROWS * GRID_W
NA_TK = NA_KROWS * GRID_W
NA_BLOCKS_PER_STEP = 4


def _na_kernel(q_ref, k_ref, vt_ref, ck_ref, cvt_ref, bias_ref, o_ref, s_scr, *, rows, cache_len):
    i = pl.program_id(1)
    nq = rows // NA_QROWS
    r0s, kinds = [], []
    for u in range(NA_BLOCKS_PER_STEP):
        qi = i * NA_BLOCKS_PER_STEP + u
        kbase = jnp.clip(qi * NA_QROWS - NA_KH // 2, 0, rows - NA_KROWS)
        r0s.append(pl.multiple_of(kbase * GRID_W, 2 * GRID_W))
        kinds.append(jnp.where(qi == 0, 0, jnp.where(qi == nq - 1, 2, 1)))
    steps = [(u, h) for u in range(NA_BLOCKS_PER_STEP) for h in range(NA_HEADS)]

    def logits(t):
        u, h = steps[t]
        lanes = slice(h * HEAD_DIM, (h + 1) * HEAD_DIM)
        qh = q_ref[u * NA_TQ:(u + 1) * NA_TQ, lanes]
        s_win = _dot_nt(k_ref[pl.ds(r0s[u], NA_TK), lanes], qh) + bias_ref[kinds[u], h]
        s_ctx = _dot_nt(ck_ref[:, lanes], qh)
        s_scr[t % 2, 0:NA_TK, :] = s_win
        s_scr[t % 2, NA_TK:NA_TK + cache_len, :] = s_ctx
        return jnp.maximum(jnp.max(s_win, axis=0, keepdims=True), jnp.max(s_ctx, axis=0, keepdims=True))

    m = logits(0)
    heads_t = []
    for t, (u, h) in enumerate(steps):
        nxt = logits(t + 1) if t + 1 < len(steps) else None
        vrows = slice(h * VT_ROWS, (h + 1) * VT_ROWS)
        p_win = jnp.exp2(s_scr[t % 2, 0:NA_TK, :] - m).astype(BF16)
        p_ctx = jnp.exp2(s_scr[t % 2, NA_TK:NA_TK + cache_len, :] - m).astype(BF16)
        acc = _dot(vt_ref[vrows, pl.ds(r0s[u], NA_TK)], p_win) + _dot(cvt_ref[vrows, :], p_ctx)
        heads_t.append(acc[0:HEAD_DIM] / acc[HEAD_DIM:HEAD_DIM + 1])
        if h == NA_HEADS - 1:
            o_ref[u * NA_TQ:(u + 1) * NA_TQ, :] = jnp.concatenate(heads_t, axis=0).T
            heads_t = []
        m = nxt


def _na_attention(q, k, vt, cache_k, cache_vt, bias, *, batch, seq):
    n = batch * seq
    rows = seq // GRID_W
    nq = rows // NA_QROWS
    cache_len = cache_k.shape[1]
    vt_rows = NA_HEADS * VT_ROWS

    nstep = nq // NA_BLOCKS_PER_STEP
    tq = NA_BLOCKS_PER_STEP * NA_TQ
    return pl.pallas_call(
        functools.partial(_na_kernel, rows=rows, cache_len=cache_len),
        grid=(batch, nstep),
        in_specs=[
            pl.BlockSpec((tq, BRANCH_W), lambda b, i: (b * nstep + i, 0)),
            pl.BlockSpec((None, seq, BRANCH_W), lambda b, i: (b, 0, 0)),
            pl.BlockSpec((None, vt_rows, seq), lambda b, i: (b, 0, 0)),
            pl.BlockSpec((None, cache_len, BRANCH_W), lambda b, i: (b, 0, 0)),
            pl.BlockSpec((None, vt_rows, cache_len), lambda b, i: (b, 0, 0)),
            _full((3, NA_HEADS, NA_TK, NA_TQ)),
        ],
        out_specs=pl.BlockSpec((tq, BRANCH_W), lambda b, i: (b * nstep + i, 0)),
        out_shape=jax.ShapeDtypeStruct((n, BRANCH_W), F32),
        scratch_shapes=[pltpu.VMEM((2, NA_TK + cache_len, NA_TQ), F32)],
        compiler_params=_params("parallel", "arbitrary"),
        name="na_latent",
    )(q, k.reshape(batch, seq, BRANCH_W), vt, cache_k, cache_vt, bias)


def _na_bias_tables(rel_bias, rows):
    nq = rows // NA_QROWS
    kh = min(NA_KH, rows)
    n_dr, n_dc = 2 * NA_KH - 1, 2 * NA_KW - 1
    cq = np.arange(GRID_W)
    start_c = np.clip(cq - NA_KW // 2, 0, GRID_W - NA_KW)
    col_ok = (cq[None, :] >= start_c[:, None]) & (cq[None, :] < start_c[:, None] + NA_KW)
    dc = np.clip(cq[None, :] - cq[:, None], -(NA_KW - 1), NA_KW - 1) + (NA_KW - 1)
    lead = rel_bias.shape[:-2]
    onehot = (dc.T.reshape(1, -1) == np.arange(n_dc)[:, None]).astype(np.float32)
    per_dr = jnp.einsum("...rc,cx->...rx", rel_bias.astype(F32), jnp.asarray(onehot),
                        precision=lax.Precision.HIGHEST)
    per_dr = jnp.where(col_ok.T.reshape(-1), per_dr * math.log2(math.e), NEG_INF)
    masked = jnp.full((*lead, 1, GRID_W * GRID_W), NEG_INF, F32)
    per_dr = jnp.concatenate([per_dr, masked], axis=-2).reshape(*lead, n_dr + 1, GRID_W, GRID_W)
    tabs = []
    for blk in (0, 1, nq - 1):
        kbase = int(np.clip(blk * NA_QROWS - NA_KH // 2, 0, rows - NA_KROWS))
        r = blk * NA_QROWS + np.arange(NA_QROWS)
        start_r = np.clip(r - kh // 2, 0, rows - kh)
        kr = kbase + np.arange(NA_KROWS)
        row_ok = (kr[None, :] >= start_r[:, None]) & (kr[None, :] < start_r[:, None] + kh)
        dr = np.where(row_ok, kr[None, :] - r[:, None] + (NA_KH - 1), n_dr)
        tabs.append(jnp.concatenate(
            [jnp.concatenate([per_dr[..., int(dr[a, b]), :, :] for a in range(NA_QROWS)], axis=-1)
             for b in range(NA_KROWS)], axis=-2))
    return jnp.stack(tabs, axis=-4)


def _gelu_tanh(y):
    return 0.5 * y * (1.0 + jnp.tanh(math.sqrt(2.0 / math.pi) * (y + 0.044715 * (y * y * y))))


S5_PARTS = 1


def _aligned(offset, multiple):
    return offset if isinstance(offset, int) else pl.multiple_of(offset, multiple)


def _s5_kernel(x0_ref, xn_ref, bmat_ref, cmat_ref, abar_ref, s0_ref, *rest, steps, reverse, nchunk):
    if reverse:
        xprev_ref, yprev_ref, dvec_ref, wglu_ref, bglu_ref, y_ref, sfin_ref, bu0, bu1, sb0, sb1, st_ref = rest
    else:
        y_ref, sfin_ref, bu0, bu1, sb0, sb1, st_ref = rest
    j = pl.program_id(1)

    @pl.when(j == 0)
    def _():
        st_ref[...] = s0_ref[...]
        sb0[...] = jnp.zeros_like(sb0)
        bu0[...] = _dot(x0_ref[...].astype(BF16), bmat_ref[...])

    pair_rows = 2 * SUBLANES
    part_rows = steps * SUBLANES // S5_PARTS
    pairs = part_rows // pair_rows

    def stage(bu_cur, bu_nxt, sb_prev, sb_cur):
        a_re = jnp.broadcast_to(abar_ref[:, 0:SSM_N], (SUBLANES, SSM_N))
        a_im = jnp.broadcast_to(abar_ref[:, SSM_N:], (SUBLANES, SSM_N))
        old_re, old_im = st_ref[:, 0:SSM_N], st_ref[:, SSM_N:]

        def part(q, carry):
            s_re, s_im = carry
            mrows = pl.ds(_aligned(q * part_rows, part_rows), part_rows)
            y = _dot(sb_prev[mrows, :], cmat_ref[...])
            bu_nxt[mrows, :] = _dot(xn_ref[mrows, :].astype(BF16), bmat_ref[...])
            if reverse:
                y = yprev_ref[mrows, :] + y + dvec_ref[...] * xprev_ref[mrows, :]
                y = _gelu_tanh(y)
                y = y * jax.nn.sigmoid(_dot(y.astype(BF16), wglu_ref[...]) + bglu_ref[...])
            y_ref[mrows, :] = y

            base = ((S5_PARTS - 1 - q) if reverse else q) * part_rows
            for i in range(pairs):
                r0 = _aligned(base + ((pairs - 1 - i) if reverse else i) * pair_rows, pair_rows)
                out_re, out_im = [None, None], [None, None]
                for half in ((1, 0) if reverse else (0, 1)):
                    rr = _aligned(r0 + half * SUBLANES, SUBLANES)
                    n_re = a_re * s_re - a_im * s_im + bu_cur[pl.ds(rr, SUBLANES), 0:SSM_N]
                    n_im = a_re * s_im + a_im * s_re + bu_cur[pl.ds(rr, SUBLANES), SSM_N:]
                    out_re[half], out_im[half] = n_re, n_im
                    s_re, s_im = n_re, n_im
                sb_cur[pl.ds(r0, pair_rows), 0:SSM_N] = jnp.concatenate(out_re, axis=0).astype(BF16)
                sb_cur[pl.ds(r0, pair_rows), SSM_N:] = jnp.concatenate(out_im, axis=0).astype(BF16)
            return s_re, s_im

        if S5_PARTS == 1:
            s_re, s_im = part(0, (old_re, old_im))
        else:
            s_re, s_im = lax.fori_loop(0, S5_PARTS, part, (old_re, old_im))
        live = j < nchunk
        st_ref[:, 0:SSM_N] = jnp.where(live, s_re, old_re)
        st_ref[:, SSM_N:] = jnp.where(live, s_im, old_im)

    @pl.when(j % 2 == 0)
    def _():
        stage(bu0, bu1, sb0, sb1)

    @pl.when(j % 2 == 1)
    def _():
        stage(bu1, bu0, sb1, sb0)

    sfin_ref[...] = st_ref[...]


def _s5_pass(x3, sp, s0, yprev, lw, *, steps, reverse):
    halves, rows, _ = x3.shape
    nchunk = rows // (steps * SUBLANES)
    tr = steps * SUBLANES

    def chunk_of(step):
        step = jnp.clip(step, 0, nchunk - 1)
        return (nchunk - 1 - step) if reverse else step

    first = lambda h, j: (h, chunk_of(0), 0)
    nxt = lambda h, j: (h, chunk_of(j + 1), 0)
    prev = lambda h, j: (h, chunk_of(j - 1), 0)

    in_specs = [
        pl.BlockSpec((None, tr, BRANCH_W), first),
        pl.BlockSpec((None, tr, BRANCH_W), nxt),
        _full((BRANCH_W, 2 * SSM_N)),
        _full((2 * SSM_N, BRANCH_W)),
        _full((1, 2 * SSM_N)),
        pl.BlockSpec((None, SUBLANES, 2 * SSM_N), lambda h, j: (h, 0, 0)),
    ]
    args = [x3, x3, sp["bmat"], sp["cmat"], sp["abar"], s0]
    if reverse:
        in_specs += [pl.BlockSpec((None, tr, BRANCH_W), prev), pl.BlockSpec((None, tr, BRANCH_W), prev),
                     _full((1, BRANCH_W)), _full((BRANCH_W, BRANCH_W)), _full((1, BRANCH_W))]
        args += [x3, yprev, lw["ssm_d"], lw["w_glu"], lw["b_glu"]]
    return pl.pallas_call(
        functools.partial(_s5_kernel, steps=steps, reverse=reverse, nchunk=nchunk),
        grid=(halves, nchunk + 1),
        in_specs=in_specs,
        out_specs=[pl.BlockSpec((None, tr, BRANCH_W), prev),
                   pl.BlockSpec((None, SUBLANES, 2 * SSM_N), lambda h, j: (h, 0, 0))],
        out_shape=[jax.ShapeDtypeStruct(x3.shape, F32),
                   jax.ShapeDtypeStruct((halves, SUBLANES, 2 * SSM_N), F32)],
        scratch_shapes=[pltpu.VMEM((tr, 2 * SSM_N), F32), pltpu.VMEM((tr, 2 * SSM_N), F32),
                        pltpu.VMEM((tr, 2 * SSM_N), BF16), pltpu.VMEM((tr, 2 * SSM_N), BF16),
                        pltpu.VMEM((SUBLANES, 2 * SSM_N), F32)],
        compiler_params=_params("parallel", "arbitrary"),
        name="s5_reverse" if reverse else "s5_forward",
    )(*args)


def _s5_discretise(a_re, a_im, log_dt, b_re, b_im, c_re, c_im):
    lead = a_re.shape[:-2]
    dt = jnp.exp(log_dt)[..., None]
    mag = jnp.exp(dt * a_re)
    ab_re, ab_im = mag * jnp.cos(dt * a_im), mag * jnp.sin(dt * a_im)
    den = a_re * a_re + a_im * a_im
    nr, ni = ab_re - 1.0, ab_im
    coef_re = ((nr * a_re + ni * a_im) / den)[..., None]
    coef_im = ((ni * a_re - nr * a_im) / den)[..., None]
    bb_re = coef_re * b_re - coef_im * b_im
    bb_im = coef_re * b_im + coef_im * b_re
    eye = jnp.eye(SSM_G, dtype=F32)

    def in_blocks(bb):
        return jnp.einsum("...gpi,gk->...gikp", bb, eye).reshape(*lead, SSM_G * SSM_H, SSM_N)

    def out_blocks(cc):
        return jnp.einsum("...gip,gk->...gpki", cc, eye).reshape(*lead, SSM_N, SSM_G * SSM_H)

    bmat = jnp.concatenate([in_blocks(bb_re), in_blocks(bb_im)], axis=-1).astype(BF16)
    cmat = jnp.concatenate([out_blocks(c_re), -out_blocks(c_im)], axis=-2).astype(BF16)
    abar = jnp.concatenate([ab_re.reshape(*lead, 1, SSM_N), ab_im.reshape(*lead, 1, SSM_N)], axis=-1)
    return dict(bmat=bmat, cmat=cmat, abar=abar)


def _states_to_rows(s):
    b = s.shape[0]
    flat = jnp.concatenate([s[..., 0].reshape(b, DEPTH, 2, SSM_N), s[..., 1].reshape(b, DEPTH, 2, SSM_N)], axis=-1)
    return flat.transpose(1, 2, 0, 3).reshape(DEPTH, 2, b // SUBLANES, SUBLANES, 2 * SSM_N)


def _rows_to_state(r):
    b = r.shape[0] * SUBLANES
    flat = r.reshape(b, 2, SSM_G, SSM_P)
    return jnp.stack([flat[:, 0], flat[:, 1]], axis=-1)


MERGE_SUB_ROWS = 256


def _merge_kernel(x_ref, g_ref, scale_ref, shift_ref, gate_ref, ya_ref, yb_ref, yc_ref, yd_ref,
                  wz_ref, wg_ref, wbr_ref, wout_ref, fg_ref, o_ref, *, last):
    tm = x_ref.shape[0]
    sub = min(tm, MERGE_SUB_ROWS)
    for r0 in range(0, tm, sub):
        rows = slice(r0, r0 + sub)
        x = x_ref[rows, :]
        hb = _modulated_norm(x, g_ref[...], scale_ref[...], shift_ref[...]).astype(BF16)
        z = _dot(hb, wz_ref[...])
        merged = None
        for n, y_ref in enumerate((ya_ref, yb_ref, yc_ref, yd_ref)):
            zn = z[:, n * BRANCH_W:(n + 1) * BRANCH_W]
            yn = y_ref[rows, :] * (zn * jax.nn.sigmoid(zn))
            t = _dot(yn.astype(BF16), wbr_ref[n * BRANCH_W:(n + 1) * BRANCH_W, :])
            gn = _dot(hb, wg_ref[:, n * D_MODEL:(n + 1) * D_MODEL])
            term = jax.nn.sigmoid(gn) * t
            merged = term if merged is None else merged + term
        out = x + gate_ref[...] * _dot(merged.astype(BF16), wout_ref[...])
        if last:
            out = (out * lax.rsqrt(jnp.mean(out * out, axis=-1, keepdims=True) + EPS)) * fg_ref[...]
        o_ref[rows, :] = out


def _merge(x, mod_scale, mod_shift, mod_gate, ya, yb, yc3, yd, lw, final_g, *, batch, seq, tm, last, name):
    n = batch * seq
    tiles_per_seq = seq // tm
    per_batch_mod = mod_scale.shape[0] > 1

    def mod_idx(i):
        return ((i // tiles_per_seq) if per_batch_mod else 0, 0, 0)

    def yc_idx(i):
        b = i // tiles_per_seq
        return (b // SUBLANES, i % tiles_per_seq, b % SUBLANES)

    row = lambda i: (i, 0)
    ytile = pl.BlockSpec((tm, BRANCH_W), row)
    return pl.pallas_call(
        functools.partial(_merge_kernel, last=last),
        grid=(n // tm,),
        in_specs=[
            pl.BlockSpec((tm, D_MODEL), row),
            _full((1, D_MODEL)),
            pl.BlockSpec((None, 1, D_MODEL), mod_idx),
            pl.BlockSpec((None, 1, D_MODEL), mod_idx),
            pl.BlockSpec((None, 1, D_MODEL), mod_idx),
            ytile, ytile,
            pl.BlockSpec((None, tm, BRANCH_W), yc_idx),
            ytile,
            _full((D_MODEL, 4 * BRANCH_W)),
            _full((D_MODEL, 4 * D_MODEL)),
            _full((4 * BRANCH_W, D_MODEL)),
            _full((D_MODEL, D_MODEL)),
            _full((1, D_MODEL)),
        ],
        out_specs=pl.BlockSpec((tm, D_MODEL), row),
        out_shape=jax.ShapeDtypeStruct((n, D_MODEL), F32),
        compiler_params=_params("parallel"),
        name=name,
    )(x, lw["norm_g"], mod_scale, mod_shift, mod_gate, ya, yb, yc3, yd,
      lw["w_z"], lw["w_g"], lw["w_branch"], lw["w_out"], final_g)


def _rope_tables(seq):
    t = np.arange(seq)
    row = (t // GRID_W).astype(np.float32)
    col = (t % GRID_W).astype(np.float32)
    nf = HEAD_DIM // 4
    freqs = jnp.asarray(ROPE_BASE, F32) ** (-jnp.arange(nf, dtype=F32) / nf)
    ang_r = jnp.asarray(row)[:, None] * freqs[None, :]
    ang_c = jnp.asarray(col)[:, None] * freqs[None, :]
    cos = jnp.concatenate([jnp.cos(ang_r)] * 2 + [jnp.cos(ang_c)] * 2, axis=1)
    sin = jnp.concatenate([-jnp.sin(ang_r), jnp.sin(ang_r), -jnp.sin(ang_c), jnp.sin(ang_c)], axis=1)
    return jnp.tile(cos, (1, GQA_HEADS)), jnp.tile(sin, (1, GQA_HEADS))


def _all_layer_weights(norm_g, w_in, gm_v_g, gm_ws, gm_b, ssm_d, w_glu, b_glu, gqa_q_g, gqa_k_g, w_branch, w_out):
    w = w_in
    w_proj = jnp.concatenate([w[..., 0:512], w[..., 768:1536], w[..., 1792:2048], w[..., 2304:2816]],
                             axis=-1).astype(BF16)
    w_z = jnp.concatenate([w[..., 512:768], w[..., 1536:1792], w[..., 2048:2304], w[..., 2816:3072]],
                          axis=-1).astype(BF16)
    return dict(
        norm_g=norm_g.reshape(DEPTH, 1, D_MODEL),
        w_proj=w_proj,
        w_z=w_z,
        w_g=w[..., 3072:].astype(BF16),
        gm_v_g=gm_v_g.reshape(DEPTH, 1, BRANCH_W),
        gm_w=gm_ws.reshape(DEPTH, GM_GROUPS * CHUNK, CHUNK).astype(BF16),
        gm_bias=jnp.repeat(gm_b.transpose(0, 2, 1), HEAD_DIM, axis=2),
        ssm_d=ssm_d.reshape(DEPTH, 1, BRANCH_W),
        w_glu=w_glu.astype(BF16),
        b_glu=b_glu.reshape(DEPTH, 1, BRANCH_W),
        gqa_q_g=jnp.tile(gqa_q_g, (1, GQA_HEADS)).reshape(DEPTH, 1, BRANCH_W),
        gqa_k_g=jnp.tile(gqa_k_g, (1, GQA_KV)).reshape(DEPTH, 1, 128),
        w_branch=w_branch.reshape(DEPTH, 4 * BRANCH_W, D_MODEL).astype(BF16),
        w_out=w_out.astype(BF16),
    )


def _trunk_layer(x, mods, lw, sps, consts, final_g, cache, *, batch, seq, latent, last):
    scale, shift, gate = mods
    tm = 1024 if latent else 256
    ya, bq, bk, bv, cx, dq, dk, dv = _inproj(x, scale, shift, lw, consts, batch=batch, seq=seq, tm=tm, latent=latent)
    halves = batch // SUBLANES
    cx3 = cx.reshape(halves, seq * SUBLANES, BRANCH_W)
    if latent:
        yb = _na_attention(bq, bk, bv, cache["na_k"], cache["na_vt"], cache["na_bias"], batch=batch, seq=seq)
        yd = _block_attention(dq, dk, dv, cache["gqa_k"], cache["gqa_vt"], batch=batch, seq=seq,
                              kvh=GQA_KV, rep=GQA_HEADS // GQA_KV, tq=256, ck=512, name="gqa_latent")
        s0 = cache["ssm"]
    else:
        yb = _block_attention(bq, bk, _vt_layout(bv, batch, seq, NA_HEADS), None, None, batch=batch, seq=seq,
                              kvh=NA_HEADS, rep=1, tq=seq, ck=seq, name="na_ctx")
        yd = _block_attention(dq, dk, _vt_layout(dv, batch, seq, GQA_KV), None, None, batch=batch, seq=seq,
                              kvh=GQA_KV, rep=GQA_HEADS // GQA_KV, tq=seq, ck=seq, name="gqa_ctx")
        zero = jnp.zeros((halves, SUBLANES, 2 * SSM_N), F32)
        s0 = (zero, zero)
    steps = 128 if latent else 64
    yf, sf = _s5_pass(cx3, sps[0], s0[0], None, lw, steps=steps, reverse=False)
    yc3, sr = _s5_pass(cx3, sps[1], s0[1], yf, lw, steps=steps, reverse=True)
    yc3 = yc3.reshape(halves, seq, SUBLANES * BRANCH_W)
    x_new = _merge(x, scale, shift, gate, ya, yb, yc3, yd, lw, final_g, batch=batch, seq=seq, tm=tm, last=last,
                   name="merge_latent" if latent else "merge_ctx")
    return x_new, (bk, bv, dk, dv, sf, sr)


def kernel(x_prompt, x_sample, c, cache_na_k, cache_na_v, cache_gqa_k, cache_gqa_v, state_ssm, c_ctx,
           norm_g, w_ada, b_ada, w_in, gm_v_g, gm_ws, gm_b, na_rel_bias, ssm_a_re, ssm_a_im, ssm_log_dt,
           ssm_b_re, ssm_b_im, ssm_c_re, ssm_c_im, ssm_d, w_glu, b_glu, gqa_q_g, gqa_k_g, w_branch, w_out,
           final_g):
    bc, lc, _ = x_prompt.shape
    bl, ll, _ = x_sample.shape
    past = cache_na_k.shape[2]

    n_rows = 16
    cond = jnp.zeros((n_rows, D_MODEL), F32).at[0].set(c_ctx).at[1:1 + bl].set(c)
    mod = _modulation(cond, w_ada, b_ada)

    def mods_of(l, lo, hi):
        m = mod[l, lo:hi].reshape(hi - lo, 1, 3 * D_MODEL)
        return m[..., 0:D_MODEL], m[..., D_MODEL:2 * D_MODEL], m[..., 2 * D_MODEL:]

    cos, sin = _rope_tables(ll)
    ones_blk = jnp.asarray(np.kron(np.eye(BRANCH_W // HEAD_DIM), np.ones((HEAD_DIM, HEAD_DIM))), BF16)
    consts = dict(rope_cos=cos, rope_sin=sin, ones_blk=ones_blk)
    fg = final_g.reshape(1, D_MODEL)

    all_lw = _all_layer_weights(norm_g, w_in, gm_v_g, gm_ws, gm_b, ssm_d, w_glu, b_glu, gqa_q_g, gqa_k_g,
                                w_branch, w_out)
    all_sp = _s5_discretise(ssm_a_re, ssm_a_im, ssm_log_dt, ssm_b_re, ssm_b_im, ssm_c_re, ssm_c_im)
    all_bias = _na_bias_tables(na_rel_bias, ll // GRID_W)
    all_na_k = cache_na_k.transpose(1, 0, 2, 3, 4).reshape(DEPTH, bl, past, BRANCH_W).astype(BF16)
    all_gqa_k = cache_gqa_k.transpose(1, 0, 2, 3, 4).reshape(DEPTH, bl, past, GQA_KV * HEAD_DIM).astype(BF16)
    all_na_vt = _cache_vt_layout(cache_na_v)
    all_gqa_vt = _cache_vt_layout(cache_gqa_v)
    all_s0 = _states_to_rows(state_ssm)

    xp = x_prompt.reshape(bc * lc, D_MODEL)
    xs = x_sample.reshape(bl * ll, D_MODEL)
    na_k_l, na_v_l, gqa_k_l, gqa_v_l, ssm_l = [], [], [], [], []
    for l in range(DEPTH):
        lw = {name: v[l] for name, v in all_lw.items()}
        sps = [{name: v[l, d] for name, v in all_sp.items()} for d in range(2)]
        last = l == DEPTH - 1
        shift, scale, gate = mods_of(l, 0, 1)
        xp, (k_na, v_na, k_g, v_g, sf, sr) = _trunk_layer(
            xp, (scale, shift, gate), lw, sps, consts, fg, None, batch=bc, seq=lc, latent=False, last=last)
        na_k_l.append(k_na.reshape(bc, lc, NA_HEADS, HEAD_DIM))
        na_v_l.append(v_na.reshape(bc, lc, NA_HEADS, HEAD_DIM))
        gqa_k_l.append(k_g.reshape(bc, lc, GQA_KV, HEAD_DIM))
        gqa_v_l.append(v_g.reshape(bc, lc, GQA_KV, HEAD_DIM))
        ssm_l.append(jnp.stack([_rows_to_state(sf), _rows_to_state(sr)], axis=1))
        shift, scale, gate = mods_of(l, 1, 1 + bl)
        cache = dict(na_k=all_na_k[l], na_vt=all_na_vt[l], gqa_k=all_gqa_k[l], gqa_vt=all_gqa_vt[l],
                     ssm=(all_s0[l, 0], all_s0[l, 1]), na_bias=all_bias[l])
        xs, _ = _trunk_layer(xs, (scale, shift, gate), lw, sps, consts, fg, cache,
                             batch=bl, seq=ll, latent=True, last=last)
    return (xp.reshape(bc, lc, D_MODEL), xs.reshape(bl, ll, D_MODEL),
            jnp.stack(na_k_l, axis=1), jnp.stack(na_v_l, axis=1),
            jnp.stack(gqa_k_l, axis=1), jnp.stack(gqa_v_l, axis=1), jnp.stack(ssm_l, axis=1))
```

```python
import functools
import math

import numpy as np
import jax
import jax.numpy as jnp
from jax import lax
from jax.experimental import pallas as pl
from jax.experimental.pallas import tpu as pltpu

D_MODEL = 1024
DEPTH = 2
GRID_W = 64
BRANCH_W = 256
HEAD_DIM = 64
CHUNK = 128
GM_GROUPS = 4
NA_HEADS = 4
NA_KH = 8
NA_KW = 16
SSM_H = 16
SSM_G = 16
SSM_P = 64
SSM_N = SSM_G * SSM_P
GQA_HEADS = 4
GQA_KV = 2
ROPE_BASE = 10000.0
EPS = 1e-6
NEG_INF = -1e30
ATTN_SCALE = HEAD_DIM ** -0.5

SUBLANES = 8
VMEM_LIMIT = 56 * 1024 * 1024

F32 = jnp.float32
BF16 = jnp.bfloat16


def _params(*sem):
    return pltpu.CompilerParams(dimension_semantics=sem, vmem_limit_bytes=VMEM_LIMIT)


def _dot(a, b):
    return jnp.dot(a, b, preferred_element_type=F32)


def _dot_nt(a, b):
    return lax.dot_general(a, b, (((1,), (1,)), ((), ())), preferred_element_type=F32)


def _full(shape):
    nd = len(shape)
    return pl.BlockSpec(shape, lambda *_: (0,) * nd)


def _modulated_norm(x, g, scale, shift):
    y = x * lax.rsqrt(jnp.mean(x * x, axis=-1, keepdims=True) + EPS)
    return (y * g) * (1.0 + scale) + shift


def _group_sumsq(x, ones_blk):
    x2 = x * x
    hi = x2.astype(BF16)
    lo = (x2 - hi.astype(F32)).astype(BF16)
    return _dot(hi, ones_blk) + _dot(lo, ones_blk)


def _head_rmsnorm(x, ones_blk, g):
    ss = _group_sumsq(x, ones_blk)
    return (x * lax.rsqrt(ss * (1.0 / HEAD_DIM) + EPS)) * g


def _rope(x, cos, sin_signed):
    w = x.shape[1]
    lane = lax.broadcasted_iota(jnp.int32, x.shape, 1)
    up = pltpu.roll(x, 16, axis=1)
    dn = pltpu.roll(x, w - 16, axis=1)
    partner = jnp.where((lane & 16) != 0, up, dn)
    return x * cos + partner * sin_signed


VT_ROWS = HEAD_DIM + 16
QK_SCALE_LOG2 = ATTN_SCALE * math.log2(math.e)


def _vt_tile(v, heads):
    vt = v.T.astype(BF16)
    ones = jnp.ones((VT_ROWS - HEAD_DIM, v.shape[0]), BF16)
    parts = []
    for h in range(heads):
        parts += [vt[h * HEAD_DIM:(h + 1) * HEAD_DIM], ones]
    return jnp.concatenate(parts, axis=0)


def _mod_kernel(cond_ref, w_ref, b_ref, o_ref):
    c = cond_ref[...]
    s = (c * jax.nn.sigmoid(c)).astype(BF16)
    o_ref[...] = _dot(s, w_ref[...].astype(BF16)) + b_ref[...]


def _modulation(cond, w_ada, b_ada):
    r = cond.shape[0]
    nj = 3
    return pl.pallas_call(
        _mod_kernel,
        grid=(DEPTH, nj),
        in_specs=[
            pl.BlockSpec((r, D_MODEL), lambda l, j: (0, 0)),
            pl.BlockSpec((None, D_MODEL, D_MODEL), lambda l, j: (l, 0, j)),
            pl.BlockSpec((None, 1, D_MODEL), lambda l, j: (l, 0, j)),
        ],
        out_specs=pl.BlockSpec((None, r, D_MODEL), lambda l, j: (l, 0, j)),
        out_shape=jax.ShapeDtypeStruct((DEPTH, r, 3 * D_MODEL), F32),
        compiler_params=_params("arbitrary", "arbitrary"),
        name="adaln_modulation",
    )(cond, w_ada, b_ada.reshape(DEPTH, 1, 3 * D_MODEL))


INPROJ_SUB_ROWS = 256


def _inproj_kernel(x_ref, g_ref, scale_ref, shift_ref, w_ref, gmg_ref, gmw_ref, gmb_ref, ones_ref,
                   qg_ref, kg_ref, *rest, tm, latent):
    if latent:
        cos_ref, sin_ref = rest[:2]
        rest = rest[2:]
    ya_ref, bq_ref, bk_ref, bv_ref, cx_ref, dq_ref, dk_ref, dv_ref = rest

    lane_grp = lax.broadcasted_iota(jnp.int32, (CHUNK, BRANCH_W), 1) // HEAD_DIM
    ones_blk = ones_ref[...]
    sub = min(tm, INPROJ_SUB_ROWS)
    for r0 in range(0, tm, sub):
        rows = slice(r0, r0 + sub)
        h = _modulated_norm(x_ref[rows, :], g_ref[...], scale_ref[...], shift_ref[...])
        p = _dot(h.astype(BF16), w_ref[...])

        u = p[:, 0:256]
        v = p[:, 256:512]
        vn = (v * lax.rsqrt(jnp.mean(v * v, axis=-1, keepdims=True) + EPS)) * gmg_ref[...]
        vnb = vn.astype(BF16)
        for c in range(sub // CHUNK):
            r = _dot(gmw_ref[...], vnb[c * CHUNK:(c + 1) * CHUNK, :])
            sp = r[0:CHUNK]
            for g in range(1, GM_GROUPS):
                sp = jnp.where(lane_grp == g, r[g * CHUNK:(g + 1) * CHUNK], sp)
            sp = sp + gmb_ref[...]
            ya_ref[r0 + c * CHUNK:r0 + (c + 1) * CHUNK, :] = u[c * CHUNK:(c + 1) * CHUNK, :] * sp

        bq_ref[rows, :] = (p[:, 512:768] * QK_SCALE_LOG2).astype(bq_ref.dtype)
        bk_ref[rows, :] = p[:, 768:1024].astype(bk_ref.dtype)
        if latent:
            bv_ref[:, rows] = _vt_tile(p[:, 1024:1280], NA_HEADS)
        else:
            bv_ref[rows, :] = p[:, 1024:1280]
        cx_ref[rows, :] = p[:, 1280:1536]
        dq = _head_rmsnorm(p[:, 1536:1792], ones_blk, qg_ref[...])
        dk = _head_rmsnorm(p[:, 1792:1920], ones_blk[0:128, 0:128], kg_ref[...])
        if latent:
            cos = cos_ref[rows, :]
            sin = sin_ref[rows, :]
            dq = _rope(dq, cos, sin)
            dk = _rope(dk, cos[:, 0:128], sin[:, 0:128])
        dq_ref[rows, :] = (dq * QK_SCALE_LOG2).astype(dq_ref.dtype)
        dk_ref[rows, :] = dk.astype(dk_ref.dtype)
        if latent:
            dv_ref[:, rows] = _vt_tile(p[:, 1920:2048], GQA_KV)
        else:
            dv_ref[rows, :] = p[:, 1920:2048]


def _inproj(x, mod_scale, mod_shift, lw, consts, *, batch, seq, tm, latent):
    n = batch * seq
    tiles_per_seq = seq // tm
    halves = batch // SUBLANES
    per_batch_mod = mod_scale.shape[0] > 1

    def mod_idx(i):
        return ((i // tiles_per_seq) if per_batch_mod else 0, 0, 0)

    row = lambda i: (i, 0)
    kv_dtype = BF16 if latent else F32
    in_specs = [
        pl.BlockSpec((tm, D_MODEL), row),
        _full((1, D_MODEL)),
        pl.BlockSpec((None, 1, D_MODEL), mod_idx),
        pl.BlockSpec((None, 1, D_MODEL), mod_idx),
        _full((D_MODEL, 2048)),
        _full((1, BRANCH_W)),
        _full((GM_GROUPS * CHUNK, CHUNK)),
        _full((CHUNK, BRANCH_W)),
        _full((BRANCH_W, BRANCH_W)),
        _full((1, BRANCH_W)),
        _full((1, 128)),
    ]
    args = [x, lw["norm_g"], mod_scale, mod_shift, lw["w_proj"], lw["gm_v_g"], lw["gm_w"], lw["gm_bias"],
            consts["ones_blk"], lw["gqa_q_g"], lw["gqa_k_g"]]
    if latent:
        rope_idx = lambda i: (i % tiles_per_seq, 0)
        in_specs += [pl.BlockSpec((tm, BRANCH_W), rope_idx), pl.BlockSpec((tm, BRANCH_W), rope_idx)]
        args += [consts["rope_cos"], consts["rope_sin"]]

    def cx_idx(i):
        b = i // tiles_per_seq
        return (b // SUBLANES, i % tiles_per_seq, b % SUBLANES)

    def vt_spec(heads):
        return pl.BlockSpec((None, heads * VT_ROWS, tm), lambda i: (i // tiles_per_seq, 0, i % tiles_per_seq))

    def vt_shape(heads):
        return jax.ShapeDtypeStruct((batch, heads * VT_ROWS, seq), BF16)

    out_specs = [
        pl.BlockSpec((tm, BRANCH_W), row),
        pl.BlockSpec((tm, BRANCH_W), row),
        pl.BlockSpec((tm, BRANCH_W), row),
        vt_spec(NA_HEADS) if latent else pl.BlockSpec((tm, BRANCH_W), row),
        pl.BlockSpec((None, tm, BRANCH_W), cx_idx),
        pl.BlockSpec((tm, BRANCH_W), row),
        pl.BlockSpec((tm, 128), row),
        vt_spec(GQA_KV) if latent else pl.BlockSpec((tm, 128), row),
    ]
    out_shape = [
        jax.ShapeDtypeStruct((n, BRANCH_W), F32),
        jax.ShapeDtypeStruct((n, BRANCH_W), BF16),
        jax.ShapeDtypeStruct((n, BRANCH_W), kv_dtype),
        vt_shape(NA_HEADS) if latent else jax.ShapeDtypeStruct((n, BRANCH_W), F32),
        jax.ShapeDtypeStruct((halves, seq, SUBLANES * BRANCH_W), F32),
        jax.ShapeDtypeStruct((n, BRANCH_W), BF16),
        jax.ShapeDtypeStruct((n, 128), kv_dtype),
        vt_shape(GQA_KV) if latent else jax.ShapeDtypeStruct((n, 128), F32),
    ]
    return pl.pallas_call(
        functools.partial(_inproj_kernel, tm=tm, latent=latent),
        grid=(n // tm,),
        in_specs=in_specs,
        out_specs=out_specs,
        out_shape=out_shape,
        compiler_params=_params("parallel"),
        name="inproj_latent" if latent else "inproj_ctx",
    )(*args)


def _block_attn_kernel(q_ref, k_ref, vt_ref, *rest, tq, kvh, rep, lk, ck, cache_len):
    if cache_len:
        ck_ref, cvt_ref, o_ref, s_scr = rest
    else:
        o_ref, s_scr = rest
    steps = []
    for j in range(kvh):
        steps += [(j, c, False) for c in range(lk // ck)]
        if cache_len:
            steps.append((j, 0, True))
    per_head = len(steps) // kvh
    qs = [jnp.concatenate([q_ref[:, (j * rep + g) * HEAD_DIM:(j * rep + g + 1) * HEAD_DIM] for g in range(rep)],
                          axis=0) for j in range(kvh)]

    def logits(t):
        j, c, cached = steps[t]
        lanes = slice(j * HEAD_DIM, (j + 1) * HEAD_DIM)
        kc = ck_ref[:, lanes] if cached else k_ref[c * ck:(c + 1) * ck, lanes]
        st = _dot_nt(kc.astype(BF16), qs[j])
        s_scr[t % 2, 0:st.shape[0], :] = st
        return jnp.max(st, axis=0, keepdims=True)

    cmax = logits(0)
    m = acc = None
    heads_t = []
    for t, (j, c, cached) in enumerate(steps):
        nxt = logits(t + 1) if t + 1 < len(steps) else None
        n_keys = cache_len if cached else ck
        vrows = slice(j * VT_ROWS, (j + 1) * VT_ROWS)
        vt = cvt_ref[vrows, :] if cached else vt_ref[vrows, c * ck:(c + 1) * ck]
        st = s_scr[t % 2, 0:n_keys, :]
        if t % per_head == 0:
            m = cmax
            acc = _dot(vt, jnp.exp2(st - m).astype(BF16))
        else:
            m_new = jnp.maximum(m, cmax)
            acc = jnp.exp2(m - m_new) * acc + _dot(vt, jnp.exp2(st - m_new).astype(BF16))
            m = m_new
        if t % per_head == per_head - 1:
            o = acc[0:HEAD_DIM] / acc[HEAD_DIM:HEAD_DIM + 1]
            heads_t += [o[:, g * tq:(g + 1) * tq] for g in range(rep)]
        cmax = nxt
    o_ref[...] = jnp.concatenate(heads_t, axis=0).T


def _vt_layout(v, batch, seq, kvh):
    vt = v.reshape(batch, seq, kvh, HEAD_DIM).transpose(0, 2, 3, 1).astype(BF16)
    ones = jnp.ones((batch, kvh, VT_ROWS - HEAD_DIM, seq), BF16)
    return jnp.concatenate([vt, ones], axis=2).reshape(batch, kvh * VT_ROWS, seq)


def _cache_vt_layout(v):
    b, _, past, heads, _ = v.shape
    vt = v.transpose(1, 0, 3, 4, 2).astype(BF16)
    ones = jnp.ones((DEPTH, b, heads, VT_ROWS - HEAD_DIM, past), BF16)
    return jnp.concatenate([vt, ones], axis=3).reshape(DEPTH, b, heads * VT_ROWS, past)


def _block_attention(q, k, vt, cache_k, cache_vt, *, batch, seq, kvh, rep, tq, ck, name):
    n = batch * seq
    wkv = kvh * HEAD_DIM
    nq = seq // tq
    cache_len = 0 if cache_k is None else cache_k.shape[1]
    in_specs = [
        pl.BlockSpec((tq, BRANCH_W), lambda b, i: (b * nq + i, 0)),
        pl.BlockSpec((None, seq, wkv), lambda b, i: (b, 0, 0)),
        pl.BlockSpec((None, kvh * VT_ROWS, seq), lambda b, i: (b, 0, 0)),
    ]
    args = [q, k.reshape(batch, seq, wkv), vt]
    if cache_len:
        in_specs += [pl.BlockSpec((None, cache_len, wkv), lambda b, i: (b, 0, 0)),
                     pl.BlockSpec((None, kvh * VT_ROWS, cache_len), lambda b, i: (b, 0, 0))]
        args += [cache_k, cache_vt]
    return pl.pallas_call(
        functools.partial(_block_attn_kernel, tq=tq, kvh=kvh, rep=rep, lk=seq, ck=ck, cache_len=cache_len),
        grid=(batch, nq),
        in_specs=in_specs,
        out_specs=pl.BlockSpec((tq, BRANCH_W), lambda b, i: (b * nq + i, 0)),
        out_shape=jax.ShapeDtypeStruct((n, BRANCH_W), F32),
        scratch_shapes=[pltpu.VMEM((2, max(ck, cache_len), rep * tq), F32)],
        compiler_params=_params("parallel", "arbitrary"),
        name=name,
    )(*args)


NA_QROWS = 4
NA_KROWS = 12
NA_TQ = NA_QROWS * GRID_W
NA_TK = NA_KROWS * GRID_W
NA_BLOCKS_PER_STEP = 4


def _na_kernel(q_ref, k_ref, vt_ref, ck_ref, cvt_ref, bias_ref, o_ref, s_scr, *, rows, cache_len):
    i = pl.program_id(1)
    nq = rows // NA_QROWS
    r0s, kinds = [], []
    for u in range(NA_BLOCKS_PER_STEP):
        qi = i * NA_BLOCKS_PER_STEP + u
        kbase = jnp.clip(qi * NA_QROWS - NA_KH // 2, 0, rows - NA_KROWS)
        r0s.append(pl.multiple_of(kbase * GRID_W, 2 * GRID_W))
        kinds.append(jnp.where(qi == 0, 0, jnp.where(qi == nq - 1, 2, 1)))
    steps = [(u, h) for u in range(NA_BLOCKS_PER_STEP) for h in range(NA_HEADS)]

    def logits(t):
        u, h = steps[t]
        lanes = slice(h * HEAD_DIM, (h + 1) * HEAD_DIM)
        qh = q_ref[u * NA_TQ:(u + 1) * NA_TQ, lanes]
        s_win = _dot_nt(k_ref[pl.ds(r0s[u], NA_TK), lanes], qh) + bias_ref[kinds[u], h]
        s_ctx = _dot_nt(ck_ref[:, lanes], qh)
        s_scr[t % 2, 0:NA_TK, :] = s_win
        s_scr[t % 2, NA_TK:NA_TK + cache_len, :] = s_ctx
        return jnp.maximum(jnp.max(s_win, axis=0, keepdims=True), jnp.max(s_ctx, axis=0, keepdims=True))

    m = logits(0)
    heads_t = []
    for t, (u, h) in enumerate(steps):
        nxt = logits(t + 1) if t + 1 < len(steps) else None
        vrows = slice(h * VT_ROWS, (h + 1) * VT_ROWS)
        p_win = jnp.exp2(s_scr[t % 2, 0:NA_TK, :] - m).astype(BF16)
        p_ctx = jnp.exp2(s_scr[t % 2, NA_TK:NA_TK + cache_len, :] - m).astype(BF16)
        acc = _dot(vt_ref[vrows, pl.ds(r0s[u], NA_TK)], p_win) + _dot(cvt_ref[vrows, :], p_ctx)
        heads_t.append(acc[0:HEAD_DIM] / acc[HEAD_DIM:HEAD_DIM + 1])
        if h == NA_HEADS - 1:
            o_ref[u * NA_TQ:(u + 1) * NA_TQ, :] = jnp.concatenate(heads_t, axis=0).T
            heads_t = []
        m = nxt


def _na_attention(q, k, vt, cache_k, cache_vt, bias, *, batch, seq):
    n = batch * seq
    rows = seq // GRID_W
    nq = rows // NA_QROWS
    cache_len = cache_k.shape[1]
    vt_rows = NA_HEADS * VT_ROWS

    nstep = nq // NA_BLOCKS_PER_STEP
    tq = NA_BLOCKS_PER_STEP * NA_TQ
    return pl.pallas_call(
        functools.partial(_na_kernel, rows=rows, cache_len=cache_len),
        grid=(batch, nstep),
        in_specs=[
            pl.BlockSpec((tq, BRANCH_W), lambda b, i: (b * nstep + i, 0)),
            pl.BlockSpec((None, seq, BRANCH_W), lambda b, i: (b, 0, 0)),
            pl.BlockSpec((None, vt_rows, seq), lambda b, i: (b, 0, 0)),
            pl.BlockSpec((None, cache_len, BRANCH_W), lambda b, i: (b, 0, 0)),
            pl.BlockSpec((None, vt_rows, cache_len), lambda b, i: (b, 0, 0)),
            _full((3, NA_HEADS, NA_TK, NA_TQ)),
        ],
        out_specs=pl.BlockSpec((tq, BRANCH_W), lambda b, i: (b * nstep + i, 0)),
        out_shape=jax.ShapeDtypeStruct((n, BRANCH_W), F32),
        scratch_shapes=[pltpu.VMEM((2, NA_TK + cache_len, NA_TQ), F32)],
        compiler_params=_params("parallel", "arbitrary"),
        name="na_latent",
    )(q, k.reshape(batch, seq, BRANCH_W), vt, cache_k, cache_vt, bias)


def _na_bias_tables(rel_bias, rows):
    nq = rows // NA_QROWS
    kh = min(NA_KH, rows)
    n_dr, n_dc = 2 * NA_KH - 1, 2 * NA_KW - 1
    cq = np.arange(GRID_W)
    start_c = np.clip(cq - NA_KW // 2, 0, GRID_W - NA_KW)
    col_ok = (cq[None, :] >= start_c[:, None]) & (cq[None, :] < start_c[:, None] + NA_KW)
    dc = np.clip(cq[None, :] - cq[:, None], -(NA_KW - 1), NA_KW - 1) + (NA_KW - 1)
    lead = rel_bias.shape[:-2]
    onehot = (dc.T.reshape(1, -1) == np.arange(n_dc)[:, None]).astype(np.float32)
    per_dr = jnp.einsum("...rc,cx->...rx", rel_bias.astype(F32), jnp.asarray(onehot),
                        precision=lax.Precision.HIGHEST)
    per_dr = jnp.where(col_ok.T.reshape(-1), per_dr * math.log2(math.e), NEG_INF)
    masked = jnp.full((*lead, 1, GRID_W * GRID_W), NEG_INF, F32)
    per_dr = jnp.concatenate([per_dr, masked], axis=-2).reshape(*lead, n_dr + 1, GRID_W, GRID_W)
    tabs = []
    for blk in (0, 1, nq - 1):
        kbase = int(np.clip(blk * NA_QROWS - NA_KH // 2, 0, rows - NA_KROWS))
        r = blk * NA_QROWS + np.arange(NA_QROWS)
        start_r = np.clip(r - kh // 2, 0, rows - kh)
        kr = kbase + np.arange(NA_KROWS)
        row_ok = (kr[None, :] >= start_r[:, None]) & (kr[None, :] < start_r[:, None] + kh)
        dr = np.where(row_ok, kr[None, :] - r[:, None] + (NA_KH - 1), n_dr)
        tabs.append(jnp.concatenate(
            [jnp.concatenate([per_dr[..., int(dr[a, b]), :, :] for a in range(NA_QROWS)], axis=-1)
             for b in range(NA_KROWS)], axis=-2))
    return jnp.stack(tabs, axis=-4)


def _gelu_tanh(y):
    return 0.5 * y * (1.0 + jnp.tanh(math.sqrt(2.0 / math.pi) * (y + 0.044715 * (y * y * y))))


S5_PARTS = 1


def _aligned(offset, multiple):
    return offset if isinstance(offset, int) else pl.multiple_of(offset, multiple)


def _s5_kernel(x0_ref, xn_ref, bmat_ref, cmat_ref, abar_ref, s0_ref, *rest, steps, reverse, nchunk):
    if reverse:
        xprev_ref, yprev_ref, dvec_ref, wglu_ref, bglu_ref, y_ref, sfin_ref, bu0, bu1, sb0, sb1, st_ref = rest
    else:
        y_ref, sfin_ref, bu0, bu1, sb0, sb1, st_ref = rest
    j = pl.program_id(1)

    @pl.when(j == 0)
    def _():
        st_ref[...] = s0_ref[...]
        sb0[...] = jnp.zeros_like(sb0)
        bu0[...] = _dot(x0_ref[...].astype(BF16), bmat_ref[...])

    pair_rows = 2 * SUBLANES
    part_rows = steps * SUBLANES // S5_PARTS
    pairs = part_rows // pair_rows

    def stage(bu_cur, bu_nxt, sb_prev, sb_cur):
        a_re = jnp.broadcast_to(abar_ref[:, 0:SSM_N], (SUBLANES, SSM_N))
        a_im = jnp.broadcast_to(abar_ref[:, SSM_N:], (SUBLANES, SSM_N))
        old_re, old_im = st_ref[:, 0:SSM_N], st_ref[:, SSM_N:]

        def part(q, carry):
            s_re, s_im = carry
            mrows = pl.ds(_aligned(q * part_rows, part_rows), part_rows)
            y = _dot(sb_prev[mrows, :], cmat_ref[...])
            bu_nxt[mrows, :] = _dot(xn_ref[mrows, :].astype(BF16), bmat_ref[...])
            if reverse:
                y = yprev_ref[mrows, :] + y + dvec_ref[...] * xprev_ref[mrows, :]
                y = _gelu_tanh(y)
                y = y * jax.nn.sigmoid(_dot(y.astype(BF16), wglu_ref[...]) + bglu_ref[...])
            y_ref[mrows, :] = y

            base = ((S5_PARTS - 1 - q) if reverse else q) * part_rows
            for i in range(pairs):
                r0 = _aligned(base + ((pairs - 1 - i) if reverse else i) * pair_rows, pair_rows)
                out_re, out_im = [None, None], [None, None]
                for half in ((1, 0) if reverse else (0, 1)):
                    rr = _aligned(r0 + half * SUBLANES, SUBLANES)
                    n_re = a_re * s_re - a_im * s_im + bu_cur[pl.ds(rr, SUBLANES), 0:SSM_N]
                    n_im = a_re * s_im + a_im * s_re + bu_cur[pl.ds(rr, SUBLANES), SSM_N:]
                    out_re[half], out_im[half] = n_re, n_im
                    s_re, s_im = n_re, n_im
                sb_cur[pl.ds(r0, pair_rows), 0:SSM_N] = jnp.concatenate(out_re, axis=0).astype(BF16)
                sb_cur[pl.ds(r0, pair_rows), SSM_N:] = jnp.concatenate(out_im, axis=0).astype(BF16)
            return s_re, s_im

        if S5_PARTS == 1:
            s_re, s_im = part(0, (old_re, old_im))
        else:
            s_re, s_im = lax.fori_loop(0, S5_PARTS, part, (old_re, old_im))
        live = j < nchunk
        st_ref[:, 0:SSM_N] = jnp.where(live, s_re, old_re)
        st_ref[:, SSM_N:] = jnp.where(live, s_im, old_im)

    @pl.when(j % 2 == 0)
    def _():
        stage(bu0, bu1, sb0, sb1)

    @pl.when(j % 2 == 1)
    def _():
        stage(bu1, bu0, sb1, sb0)

    sfin_ref[...] = st_ref[...]


def _s5_pass(x3, sp, s0, yprev, lw, *, steps, reverse):
    halves, rows, _ = x3.shape
    nchunk = rows // (steps * SUBLANES)
    tr = steps * SUBLANES

    def chunk_of(step):
        step = jnp.clip(step, 0, nchunk - 1)
        return (nchunk - 1 - step) if reverse else step

    first = lambda h, j: (h, chunk_of(0), 0)
    nxt = lambda h, j: (h, chunk_of(j + 1), 0)
    prev = lambda h, j: (h, chunk_of(j - 1), 0)

    in_specs = [
        pl.BlockSpec((None, tr, BRANCH_W), first),
        pl.BlockSpec((None, tr, BRANCH_W), nxt),
        _full((BRANCH_W, 2 * SSM_N)),
        _full((2 * SSM_N, BRANCH_W)),
        _full((1, 2 * SSM_N)),
        pl.BlockSpec((None, SUBLANES, 2 * SSM_N), lambda h, j: (h, 0, 0)),
    ]
    args = [x3, x3, sp["bmat"], sp["cmat"], sp["abar"], s0]
    if reverse:
        in_specs += [pl.BlockSpec((None, tr, BRANCH_W), prev), pl.BlockSpec((None, tr, BRANCH_W), prev),
                     _full((1, BRANCH_W)), _full((BRANCH_W, BRANCH_W)), _full((1, BRANCH_W))]
        args += [x3, yprev, lw["ssm_d"], lw["w_glu"], lw["b_glu"]]
    return pl.pallas_call(
        functools.partial(_s5_kernel, steps=steps, reverse=reverse, nchunk=nchunk),
        grid=(halves, nchunk + 1),
        in_specs=in_specs,
        out_specs=[pl.BlockSpec((None, tr, BRANCH_W), prev),
                   pl.BlockSpec((None, SUBLANES, 2 * SSM_N), lambda h, j: (h, 0, 0))],
        out_shape=[jax.ShapeDtypeStruct(x3.shape, F32),
                   jax.ShapeDtypeStruct((halves, SUBLANES, 2 * SSM_N), F32)],
        scratch_shapes=[pltpu.VMEM((tr, 2 * SSM_N), F32), pltpu.VMEM((tr, 2 * SSM_N), F32),
                        pltpu.VMEM((tr, 2 * SSM_N), BF16), pltpu.VMEM((tr, 2 * SSM_N), BF16),
                        pltpu.VMEM((SUBLANES, 2 * SSM_N), F32)],
        compiler_params=_params("parallel", "arbitrary"),
        name="s5_reverse" if reverse else "s5_forward",
    )(*args)


def _s5_discretise(a_re, a_im, log_dt, b_re, b_im, c_re, c_im):
    lead = a_re.shape[:-2]
    dt = jnp.exp(log_dt)[..., None]
    mag = jnp.exp(dt * a_re)
    ab_re, ab_im = mag * jnp.cos(dt * a_im), mag * jnp.sin(dt * a_im)
    den = a_re * a_re + a_im * a_im
    nr, ni = ab_re - 1.0, ab_im
    coef_re = ((nr * a_re + ni * a_im) / den)[..., None]
    coef_im = ((ni * a_re - nr * a_im) / den)[..., None]
    bb_re = coef_re * b_re - coef_im * b_im
    bb_im = coef_re * b_im + coef_im * b_re
    eye = jnp.eye(SSM_G, dtype=F32)

    def in_blocks(bb):
        return jnp.einsum("...gpi,gk->...gikp", bb, eye).reshape(*lead, SSM_G * SSM_H, SSM_N)

    def out_blocks(cc):
        return jnp.einsum("...gip,gk->...gpki", cc, eye).reshape(*lead, SSM_N, SSM_G * SSM_H)

    bmat = jnp.concatenate([in_blocks(bb_re), in_blocks(bb_im)], axis=-1).astype(BF16)
    cmat = jnp.concatenate([out_blocks(c_re), -out_blocks(c_im)], axis=-2).astype(BF16)
    abar = jnp.concatenate([ab_re.reshape(*lead, 1, SSM_N), ab_im.reshape(*lead, 1, SSM_N)], axis=-1)
    return dict(bmat=bmat, cmat=cmat, abar=abar)


def _states_to_rows(s):
    b = s.shape[0]
    flat = jnp.concatenate([s[..., 0].reshape(b, DEPTH, 2, SSM_N), s[..., 1].reshape(b, DEPTH, 2, SSM_N)], axis=-1)
    return flat.transpose(1, 2, 0, 3).reshape(DEPTH, 2, b // SUBLANES, SUBLANES, 2 * SSM_N)


def _rows_to_state(r):
    b = r.shape[0] * SUBLANES
    flat = r.reshape(b, 2, SSM_G, SSM_P)
    return jnp.stack([flat[:, 0], flat[:, 1]], axis=-1)


MERGE_SUB_ROWS = 256


def _merge_kernel(x_ref, g_ref, scale_ref, shift_ref, gate_ref, ya_ref, yb_ref, yc_ref, yd_ref,
                  wz_ref, wg_ref, wbr_ref, wout_ref, fg_ref, o_ref, *, last):
    tm = x_ref.shape[0]
    sub = min(tm, MERGE_SUB_ROWS)
    for r0 in range(0, tm, sub):
        rows = slice(r0, r0 + sub)
        x = x_ref[rows, :]
        hb = _modulated_norm(x, g_ref[...], scale_ref[...], shift_ref[...]).astype(BF16)
        z = _dot(hb, wz_ref[...])
        merged = None
        for n, y_ref in enumerate((ya_ref, yb_ref, yc_ref, yd_ref)):
            zn = z[:, n * BRANCH_W:(n + 1) * BRANCH_W]
            yn = y_ref[rows, :] * (zn * jax.nn.sigmoid(zn))
            t = _dot(yn.astype(BF16), wbr_ref[n * BRANCH_W:(n + 1) * BRANCH_W, :])
            gn = _dot(hb, wg_ref[:, n * D_MODEL:(n + 1) * D_MODEL])
            term = jax.nn.sigmoid(gn) * t
            merged = term if merged is None else merged + term
        out = x + gate_ref[...] * _dot(merged.astype(BF16), wout_ref[...])
        if last:
            out = (out * lax.rsqrt(jnp.mean(out * out, axis=-1, keepdims=True) + EPS)) * fg_ref[...]
        o_ref[rows, :] = out


def _merge(x, mod_scale, mod_shift, mod_gate, ya, yb, yc3, yd, lw, final_g, *, batch, seq, tm, last, name):
    n = batch * seq
    tiles_per_seq = seq // tm
    per_batch_mod = mod_scale.shape[0] > 1

    def mod_idx(i):
        return ((i // tiles_per_seq) if per_batch_mod else 0, 0, 0)

    def yc_idx(i):
        b = i // tiles_per_seq
        return (b // SUBLANES, i % tiles_per_seq, b % SUBLANES)

    row = lambda i: (i, 0)
    ytile = pl.BlockSpec((tm, BRANCH_W), row)
    return pl.pallas_call(
        functools.partial(_merge_kernel, last=last),
        grid=(n // tm,),
        in_specs=[
            pl.BlockSpec((tm, D_MODEL), row),
            _full((1, D_MODEL)),
            pl.BlockSpec((None, 1, D_MODEL), mod_idx),
            pl.BlockSpec((None, 1, D_MODEL), mod_idx),
            pl.BlockSpec((None, 1, D_MODEL), mod_idx),
            ytile, ytile,
            pl.BlockSpec((None, tm, BRANCH_W), yc_idx),
            ytile,
            _full((D_MODEL, 4 * BRANCH_W)),
            _full((D_MODEL, 4 * D_MODEL)),
            _full((4 * BRANCH_W, D_MODEL)),
            _full((D_MODEL, D_MODEL)),
            _full((1, D_MODEL)),
        ],
        out_specs=pl.BlockSpec((tm, D_MODEL), row),
        out_shape=jax.ShapeDtypeStruct((n, D_MODEL), F32),
        compiler_params=_params("parallel"),
        name=name,
    )(x, lw["norm_g"], mod_scale, mod_shift, mod_gate, ya, yb, yc3, yd,
      lw["w_z"], lw["w_g"], lw["w_branch"], lw["w_out"], final_g)


def _rope_tables(seq):
    t = np.arange(seq)
    row = (t // GRID_W).astype(np.float32)
    col = (t % GRID_W).astype(np.float32)
    nf = HEAD_DIM // 4
    freqs = jnp.asarray(ROPE_BASE, F32) ** (-jnp.arange(nf, dtype=F32) / nf)
    ang_r = jnp.asarray(row)[:, None] * freqs[None, :]
    ang_c = jnp.asarray(col)[:, None] * freqs[None, :]
    cos = jnp.concatenate([jnp.cos(ang_r)] * 2 + [jnp.cos(ang_c)] * 2, axis=1)
    sin = jnp.concatenate([-jnp.sin(ang_r), jnp.sin(ang_r), -jnp.sin(ang_c), jnp.sin(ang_c)], axis=1)
    return jnp.tile(cos, (1, GQA_HEADS)), jnp.tile(sin, (1, GQA_HEADS))


def _all_layer_weights(norm_g, w_in, gm_v_g, gm_ws, gm_b, ssm_d, w_glu, b_glu, gqa_q_g, gqa_k_g, w_branch, w_out):
    w = w_in
    w_proj = jnp.concatenate([w[..., 0:512], w[..., 768:1536], w[..., 1792:2048], w[..., 2304:2816]],
                             axis=-1).astype(BF16)
    w_z = jnp.concatenate([w[..., 512:768], w[..., 1536:1792], w[..., 2048:2304], w[..., 2816:3072]],
                          axis=-1).astype(BF16)
    return dict(
        norm_g=norm_g.reshape(DEPTH, 1, D_MODEL),
        w_proj=w_proj,
        w_z=w_z,
        w_g=w[..., 3072:].astype(BF16),
        gm_v_g=gm_v_g.reshape(DEPTH, 1, BRANCH_W),
        gm_w=gm_ws.reshape(DEPTH, GM_GROUPS * CHUNK, CHUNK).astype(BF16),
        gm_bias=jnp.repeat(gm_b.transpose(0, 2, 1), HEAD_DIM, axis=2),
        ssm_d=ssm_d.reshape(DEPTH, 1, BRANCH_W),
        w_glu=w_glu.astype(BF16),
        b_glu=b_glu.reshape(DEPTH, 1, BRANCH_W),
        gqa_q_g=jnp.tile(gqa_q_g, (1, GQA_HEADS)).reshape(DEPTH, 1, BRANCH_W),
        gqa_k_g=jnp.tile(gqa_k_g, (1, GQA_KV)).reshape(DEPTH, 1, 128),
        w_branch=w_branch.reshape(DEPTH, 4 * BRANCH_W, D_MODEL).astype(BF16),
        w_out=w_out.astype(BF16),
    )


def _trunk_layer(x, mods, lw, sps, consts, final_g, cache, *, batch, seq, latent, last):
    scale, shift, gate = mods
    tm = 1024 if latent else 256
    ya, bq, bk, bv, cx, dq, dk, dv = _inproj(x, scale, shift, lw, consts, batch=batch, seq=seq, tm=tm, latent=latent)
    halves = batch // SUBLANES
    cx3 = cx.reshape(halves, seq * SUBLANES, BRANCH_W)
    if latent:
        yb = _na_attention(bq, bk, bv, cache["na_k"], cache["na_vt"], cache["na_bias"], batch=batch, seq=seq)
        yd = _block_attention(dq, dk, dv, cache["gqa_k"], cache["gqa_vt"], batch=batch, seq=seq,
                              kvh=GQA_KV, rep=GQA_HEADS // GQA_KV, tq=256, ck=512, name="gqa_latent")
        s0 = cache["ssm"]
    else:
        yb = _block_attention(bq, bk, _vt_layout(bv, batch, seq, NA_HEADS), None, None, batch=batch, seq=seq,
                              kvh=NA_HEADS, rep=1, tq=seq, ck=seq, name="na_ctx")
        yd = _block_attention(dq, dk, _vt_layout(dv, batch, seq, GQA_KV), None, None, batch=batch, seq=seq,
                              kvh=GQA_KV, rep=GQA_HEADS // GQA_KV, tq=seq, ck=seq, name="gqa_ctx")
        zero = jnp.zeros((halves, SUBLANES, 2 * SSM_N), F32)
        s0 = (zero, zero)
    steps = 128 if latent else 64
    yf, sf = _s5_pass(cx3, sps[0], s0[0], None, lw, steps=steps, reverse=False)
    yc3, sr = _s5_pass(cx3, sps[1], s0[1], yf, lw, steps=steps, reverse=True)
    yc3 = yc3.reshape(halves, seq, SUBLANES * BRANCH_W)
    x_new = _merge(x, scale, shift, gate, ya, yb, yc3, yd, lw, final_g, batch=batch, seq=seq, tm=tm, last=last,
                   name="merge_latent" if latent else "merge_ctx")
    return x_new, (bk, bv, dk, dv, sf, sr)


def kernel(x_prompt, x_sample, c, cache_na_k, cache_na_v, cache_gqa_k, cache_gqa_v, state_ssm, c_ctx,
           norm_g, w_ada, b_ada, w_in, gm_v_g, gm_ws, gm_b, na_rel_bias, ssm_a_re, ssm_a_im, ssm_log_dt,
           ssm_b_re, ssm_b_im, ssm_c_re, ssm_c_im, ssm_d, w_glu, b_glu, gqa_q_g, gqa_k_g, w_branch, w_out,
           final_g):
    bc, lc, _ = x_prompt.shape
    bl, ll, _ = x_sample.shape
    past = cache_na_k.shape[2]

    n_rows = 16
    cond = jnp.zeros((n_rows, D_MODEL), F32).at[0].set(c_ctx).at[1:1 + bl].set(c)
    mod = _modulation(cond, w_ada, b_ada)

    def mods_of(l, lo, hi):
        m = mod[l, lo:hi].reshape(hi - lo, 1, 3 * D_MODEL)
        return m[..., 0:D_MODEL], m[..., D_MODEL:2 * D_MODEL], m[..., 2 * D_MODEL:]

    cos, sin = _rope_tables(ll)
    ones_blk = jnp.asarray(np.kron(np.eye(BRANCH_W // HEAD_DIM), np.ones((HEAD_DIM, HEAD_DIM))), BF16)
    consts = dict(rope_cos=cos, rope_sin=sin, ones_blk=ones_blk)
    fg = final_g.reshape(1, D_MODEL)

    all_lw = _all_layer_weights(norm_g, w_in, gm_v_g, gm_ws, gm_b, ssm_d, w_glu, b_glu, gqa_q_g, gqa_k_g,
                                w_branch, w_out)
    all_sp = _s5_discretise(ssm_a_re, ssm_a_im, ssm_log_dt, ssm_b_re, ssm_b_im, ssm_c_re, ssm_c_im)
    all_na_k = cache_na_k.transpose(1, 0, 2, 3, 4).reshape(DEPTH, bl, past, BRANCH_W).astype(BF16)
    all_gqa_k = cache_gqa_k.transpose(1, 0, 2, 3, 4).reshape(DEPTH, bl, past, GQA_KV * HEAD_DIM).astype(BF16)
    all_na_vt = _cache_vt_layout(cache_na_v)
    all_gqa_vt = _cache_vt_layout(cache_gqa_v)
    all_s0 = _states_to_rows(state_ssm)

    xp = x_prompt.reshape(bc * lc, D_MODEL)
    xs = x_sample.reshape(bl * ll, D_MODEL)
    na_k_l, na_v_l, gqa_k_l, gqa_v_l, ssm_l = [], [], [], [], []
    for l in range(DEPTH):
        lw = {name: v[l] for name, v in all_lw.items()}
        sps = [{name: v[l, d] for name, v in all_sp.items()} for d in range(2)]
        last = l == DEPTH - 1
        shift, scale, gate = mods_of(l, 0, 1)
        xp, (k_na, v_na, k_g, v_g, sf, sr) = _trunk_layer(
            xp, (scale, shift, gate), lw, sps, consts, fg, None, batch=bc, seq=lc, latent=False, last=last)
        na_k_l.append(k_na.reshape(bc, lc, NA_HEADS, HEAD_DIM))
        na_v_l.append(v_na.reshape(bc, lc, NA_HEADS, HEAD_DIM))
        gqa_k_l.append(k_g.reshape(bc, lc, GQA_KV, HEAD_DIM))
        gqa_v_l.append(v_g.reshape(bc, lc, GQA_KV, HEAD_DIM))
        ssm_l.append(jnp.stack([_rows_to_state(sf), _rows_to_state(sr)], axis=1))
        shift, scale, gate = mods_of(l, 1, 1 + bl)
        cache = dict(na_k=all_na_k[l], na_vt=all_na_vt[l], gqa_k=all_gqa_k[l], gqa_vt=all_gqa_vt[l],
                     ssm=(all_s0[l, 0], all_s0[l, 1]), na_bias=_na_bias_tables(na_rel_bias[l], ll // GRID_W))
        xs, _ = _trunk_layer(xs, (scale, shift, gate), lw, sps, consts, fg, cache,
                             batch=bl, seq=ll, latent=True, last=last)
    return (xp.reshape(bc, lc, D_MODEL), xs.reshape(bl, ll, D_MODEL),
            jnp.stack(na_k_l, axis=1), jnp.stack(na_v_l, axis=1),
            jnp.stack(gqa_k_l, axis=1), jnp.stack(gqa_v_l, axis=1), jnp.stack(ssm_l, axis=1))
```

```python
import functools
import math

import numpy as np
import jax
import jax.numpy as jnp
from jax import lax
from jax.experimental import pallas as pl
from jax.experimental.pallas import tpu as pltpu

D_MODEL = 1024
DEPTH = 2
GRID_W = 64
BRANCH_W = 256
HEAD_DIM = 64
CHUNK = 128
GM_GROUPS = 4
NA_HEADS = 4
NA_KH = 8
NA_KW = 16
SSM_H = 16
SSM_G = 16
SSM_P = 64
SSM_N = SSM_G * SSM_P
GQA_HEADS = 4
GQA_KV = 2
ROPE_BASE = 10000.0
EPS = 1e-6
NEG_INF = -1e30
ATTN_SCALE = HEAD_DIM ** -0.5

SUBLANES = 8
VMEM_LIMIT = 56 * 1024 * 1024

F32 = jnp.float32
BF16 = jnp.bfloat16


def _params(*sem):
    return pltpu.CompilerParams(dimension_semantics=sem, vmem_limit_bytes=VMEM_LIMIT)


def _dot(a, b):
    return jnp.dot(a, b, preferred_element_type=F32)


def _dot_nt(a, b):
    return lax.dot_general(a, b, (((1,), (1,)), ((), ())), preferred_element_type=F32)


def _full(shape):
    nd = len(shape)
    return pl.BlockSpec(shape, lambda *_: (0,) * nd)


def _modulated_norm(x, g, scale, shift):
    y = x * lax.rsqrt(jnp.mean(x * x, axis=-1, keepdims=True) + EPS)
    return (y * g) * (1.0 + scale) + shift


def _group_sumsq(x, ones_blk):
    x2 = x * x
    hi = x2.astype(BF16)
    lo = (x2 - hi.astype(F32)).astype(BF16)
    return _dot(hi, ones_blk) + _dot(lo, ones_blk)


def _head_rmsnorm(x, ones_blk, g):
    ss = _group_sumsq(x, ones_blk)
    return (x * lax.rsqrt(ss * (1.0 / HEAD_DIM) + EPS)) * g


def _rope(x, cos, sin_signed):
    w = x.shape[1]
    lane = lax.broadcasted_iota(jnp.int32, x.shape, 1)
    up = pltpu.roll(x, 16, axis=1)
    dn = pltpu.roll(x, w - 16, axis=1)
    partner = jnp.where((lane & 16) != 0, up, dn)
    return x * cos + partner * sin_signed


VT_ROWS = HEAD_DIM + 16
QK_SCALE_LOG2 = ATTN_SCALE * math.log2(math.e)


def _vt_tile(v, heads):
    vt = v.T.astype(BF16)
    ones = jnp.ones((VT_ROWS - HEAD_DIM, v.shape[0]), BF16)
    parts = []
    for h in range(heads):
        parts += [vt[h * HEAD_DIM:(h + 1) * HEAD_DIM], ones]
    return jnp.concatenate(parts, axis=0)


def _mod_kernel(cond_ref, w_ref, b_ref, o_ref):
    c = cond_ref[...]
    s = (c * jax.nn.sigmoid(c)).astype(BF16)
    o_ref[...] = _dot(s, w_ref[...].astype(BF16)) + b_ref[...]


def _modulation(cond, w_ada, b_ada):
    r = cond.shape[0]
    nj = 3
    return pl.pallas_call(
        _mod_kernel,
        grid=(DEPTH, nj),
        in_specs=[
            pl.BlockSpec((r, D_MODEL), lambda l, j: (0, 0)),
            pl.BlockSpec((None, D_MODEL, D_MODEL), lambda l, j: (l, 0, j)),
            pl.BlockSpec((None, 1, D_MODEL), lambda l, j: (l, 0, j)),
        ],
        out_specs=pl.BlockSpec((None, r, D_MODEL), lambda l, j: (l, 0, j)),
        out_shape=jax.ShapeDtypeStruct((DEPTH, r, 3 * D_MODEL), F32),
        compiler_params=_params("arbitrary", "arbitrary"),
        name="adaln_modulation",
    )(cond, w_ada, b_ada.reshape(DEPTH, 1, 3 * D_MODEL))


INPROJ_SUB_ROWS = 256


def _inproj_kernel(x_ref, g_ref, scale_ref, shift_ref, w_ref, gmg_ref, gmw_ref, gmb_ref, ones_ref,
                   qg_ref, kg_ref, *rest, tm, latent):
    if latent:
        cos_ref, sin_ref = rest[:2]
        rest = rest[2:]
    ya_ref, bq_ref, bk_ref, bv_ref, cx_ref, dq_ref, dk_ref, dv_ref = rest

    lane_grp = lax.broadcasted_iota(jnp.int32, (CHUNK, BRANCH_W), 1) // HEAD_DIM
    ones_blk = ones_ref[...]
    sub = min(tm, INPROJ_SUB_ROWS)
    for r0 in range(0, tm, sub):
        rows = slice(r0, r0 + sub)
        h = _modulated_norm(x_ref[rows, :], g_ref[...], scale_ref[...], shift_ref[...])
        p = _dot(h.astype(BF16), w_ref[...])

        u = p[:, 0:256]
        v = p[:, 256:512]
        vn = (v * lax.rsqrt(jnp.mean(v * v, axis=-1, keepdims=True) + EPS)) * gmg_ref[...]
        vnb = vn.astype(BF16)
        for c in range(sub // CHUNK):
            r = _dot(gmw_ref[...], vnb[c * CHUNK:(c + 1) * CHUNK, :])
            sp = r[0:CHUNK]
            for g in range(1, GM_GROUPS):
                sp = jnp.where(lane_grp == g, r[g * CHUNK:(g + 1) * CHUNK], sp)
            sp = sp + gmb_ref[...]
            ya_ref[r0 + c * CHUNK:r0 + (c + 1) * CHUNK, :] = u[c * CHUNK:(c + 1) * CHUNK, :] * sp

        bq_ref[rows, :] = (p[:, 512:768] * QK_SCALE_LOG2).astype(bq_ref.dtype)
        bk_ref[rows, :] = p[:, 768:1024].astype(bk_ref.dtype)
        if latent:
            bv_ref[:, rows] = _vt_tile(p[:, 1024:1280], NA_HEADS)
        else:
            bv_ref[rows, :] = p[:, 1024:1280]
        cx_ref[rows, :] = p[:, 1280:1536]
        dq = _head_rmsnorm(p[:, 1536:1792], ones_blk, qg_ref[...])
        dk = _head_rmsnorm(p[:, 1792:1920], ones_blk[0:128, 0:128], kg_ref[...])
        if latent:
            cos = cos_ref[rows, :]
            sin = sin_ref[rows, :]
            dq = _rope(dq, cos, sin)
            dk = _rope(dk, cos[:, 0:128], sin[:, 0:128])
        dq_ref[rows, :] = (dq * QK_SCALE_LOG2).astype(dq_ref.dtype)
        dk_ref[rows, :] = dk.astype(dk_ref.dtype)
        if latent:
            dv_ref[:, rows] = _vt_tile(p[:, 1920:2048], GQA_KV)
        else:
            dv_ref[rows, :] = p[:, 1920:2048]


def _inproj(x, mod_scale, mod_shift, lw, consts, *, batch, seq, tm, latent):
    n = batch * seq
    tiles_per_seq = seq // tm
    halves = batch // SUBLANES
    per_batch_mod = mod_scale.shape[0] > 1

    def mod_idx(i):
        return ((i // tiles_per_seq) if per_batch_mod else 0, 0, 0)

    row = lambda i: (i, 0)
    kv_dtype = BF16 if latent else F32
    in_specs = [
        pl.BlockSpec((tm, D_MODEL), row),
        _full((1, D_MODEL)),
        pl.BlockSpec((None, 1, D_MODEL), mod_idx),
        pl.BlockSpec((None, 1, D_MODEL), mod_idx),
        _full((D_MODEL, 2048)),
        _full((1, BRANCH_W)),
        _full((GM_GROUPS * CHUNK, CHUNK)),
        _full((CHUNK, BRANCH_W)),
        _full((BRANCH_W, BRANCH_W)),
        _full((1, BRANCH_W)),
        _full((1, 128)),
    ]
    args = [x, lw["norm_g"], mod_scale, mod_shift, lw["w_proj"], lw["gm_v_g"], lw["gm_w"], lw["gm_bias"],
            consts["ones_blk"], lw["gqa_q_g"], lw["gqa_k_g"]]
    if latent:
        rope_idx = lambda i: (i % tiles_per_seq, 0)
        in_specs += [pl.BlockSpec((tm, BRANCH_W), rope_idx), pl.BlockSpec((tm, BRANCH_W), rope_idx)]
        args += [consts["rope_cos"], consts["rope_sin"]]

    def cx_idx(i):
        b = i // tiles_per_seq
        return (b // SUBLANES, i % tiles_per_seq, b % SUBLANES)

    def vt_spec(heads):
        return pl.BlockSpec((None, heads * VT_ROWS, tm), lambda i: (i // tiles_per_seq, 0, i % tiles_per_seq))

    def vt_shape(heads):
        return jax.ShapeDtypeStruct((batch, heads * VT_ROWS, seq), BF16)

    out_specs = [
        pl.BlockSpec((tm, BRANCH_W), row),
        pl.BlockSpec((tm, BRANCH_W), row),
        pl.BlockSpec((tm, BRANCH_W), row),
        vt_spec(NA_HEADS) if latent else pl.BlockSpec((tm, BRANCH_W), row),
        pl.BlockSpec((None, tm, BRANCH_W), cx_idx),
        pl.BlockSpec((tm, BRANCH_W), row),
        pl.BlockSpec((tm, 128), row),
        vt_spec(GQA_KV) if latent else pl.BlockSpec((tm, 128), row),
    ]
    out_shape = [
        jax.ShapeDtypeStruct((n, BRANCH_W), F32),
        jax.ShapeDtypeStruct((n, BRANCH_W), BF16),
        jax.ShapeDtypeStruct((n, BRANCH_W), kv_dtype),
        vt_shape(NA_HEADS) if latent else jax.ShapeDtypeStruct((n, BRANCH_W), F32),
        jax.ShapeDtypeStruct((halves, seq, SUBLANES * BRANCH_W), F32),
        jax.ShapeDtypeStruct((n, BRANCH_W), BF16),
        jax.ShapeDtypeStruct((n, 128), kv_dtype),
        vt_shape(GQA_KV) if latent else jax.ShapeDtypeStruct((n, 128), F32),
    ]
    return pl.pallas_call(
        functools.partial(_inproj_kernel, tm=tm, latent=latent),
        grid=(n // tm,),
        in_specs=in_specs,
        out_specs=out_specs,
        out_shape=out_shape,
        compiler_params=_params("parallel"),
        name="inproj_latent" if latent else "inproj_ctx",
    )(*args)


def _block_attn_kernel(q_ref, k_ref, vt_ref, *rest, tq, kvh, rep, lk, ck, cache_len):
    if cache_len:
        ck_ref, cvt_ref, o_ref, s_scr = rest
    else:
        o_ref, s_scr = rest
    steps = []
    for j in range(kvh):
        steps += [(j, c, False) for c in range(lk // ck)]
        if cache_len:
            steps.append((j, 0, True))
    per_head = len(steps) // kvh
    qs = [jnp.concatenate([q_ref[:, (j * rep + g) * HEAD_DIM:(j * rep + g + 1) * HEAD_DIM] for g in range(rep)],
                          axis=0) for j in range(kvh)]

    def logits(t):
        j, c, cached = steps[t]
        lanes = slice(j * HEAD_DIM, (j + 1) * HEAD_DIM)
        kc = ck_ref[:, lanes] if cached else k_ref[c * ck:(c + 1) * ck, lanes]
        st = _dot_nt(kc.astype(BF16), qs[j])
        s_scr[t % 2, 0:st.shape[0], :] = st
        return jnp.max(st, axis=0, keepdims=True)

    cmax = logits(0)
    m = acc = None
    heads_t = []
    for t, (j, c, cached) in enumerate(steps):
        nxt = logits(t + 1) if t + 1 < len(steps) else None
        n_keys = cache_len if cached else ck
        vrows = slice(j * VT_ROWS, (j + 1) * VT_ROWS)
        vt = cvt_ref[vrows, :] if cached else vt_ref[vrows, c * ck:(c + 1) * ck]
        st = s_scr[t % 2, 0:n_keys, :]
        if t % per_head == 0:
            m = cmax
            acc = _dot(vt, jnp.exp2(st - m).astype(BF16))
        else:
            m_new = jnp.maximum(m, cmax)
            acc = jnp.exp2(m - m_new) * acc + _dot(vt, jnp.exp2(st - m_new).astype(BF16))
            m = m_new
        if t % per_head == per_head - 1:
            o = acc[0:HEAD_DIM] / acc[HEAD_DIM:HEAD_DIM + 1]
            heads_t += [o[:, g * tq:(g + 1) * tq] for g in range(rep)]
        cmax = nxt
    o_ref[...] = jnp.concatenate(heads_t, axis=0).T


def _vt_layout(v, batch, seq, kvh):
    vt = v.reshape(batch, seq, kvh, HEAD_DIM).transpose(0, 2, 3, 1).astype(BF16)
    ones = jnp.ones((batch, kvh, VT_ROWS - HEAD_DIM, seq), BF16)
    return jnp.concatenate([vt, ones], axis=2).reshape(batch, kvh * VT_ROWS, seq)


def _block_attention(q, k, vt, cache_k, cache_vt, *, batch, seq, kvh, rep, tq, ck, name):
    n = batch * seq
    wkv = kvh * HEAD_DIM
    nq = seq // tq
    cache_len = 0 if cache_k is None else cache_k.shape[1]
    in_specs = [
        pl.BlockSpec((tq, BRANCH_W), lambda b, i: (b * nq + i, 0)),
        pl.BlockSpec((None, seq, wkv), lambda b, i: (b, 0, 0)),
        pl.BlockSpec((None, kvh * VT_ROWS, seq), lambda b, i: (b, 0, 0)),
    ]
    args = [q, k.reshape(batch, seq, wkv), vt]
    if cache_len:
        in_specs += [pl.BlockSpec((None, cache_len, wkv), lambda b, i: (b, 0, 0)),
                     pl.BlockSpec((None, kvh * VT_ROWS, cache_len), lambda b, i: (b, 0, 0))]
        args += [cache_k, cache_vt]
    return pl.pallas_call(
        functools.partial(_block_attn_kernel, tq=tq, kvh=kvh, rep=rep, lk=seq, ck=ck, cache_len=cache_len),
        grid=(batch, nq),
        in_specs=in_specs,
        out_specs=pl.BlockSpec((tq, BRANCH_W), lambda b, i: (b * nq + i, 0)),
        out_shape=jax.ShapeDtypeStruct((n, BRANCH_W), F32),
        scratch_shapes=[pltpu.VMEM((2, max(ck, cache_len), rep * tq), F32)],
        compiler_params=_params("parallel", "arbitrary"),
        name=name,
    )(*args)


NA_QROWS = 4
NA_KROWS = 12
NA_TQ = NA_QROWS * GRID_W
NA_TK = NA_KROWS * GRID_W
NA_BLOCKS_PER_STEP = 8


def _na_kernel(q_ref, k_ref, vt_ref, ck_ref, cvt_ref, bias_ref, o_ref, s_scr, *, rows, cache_len):
    i = pl.program_id(1)
    nq = rows // NA_QROWS
    r0s, kinds = [], []
    for u in range(NA_BLOCKS_PER_STEP):
        qi = i * NA_BLOCKS_PER_STEP + u
        kbase = jnp.clip(qi * NA_QROWS - NA_KH // 2, 0, rows - NA_KROWS)
        r0s.append(pl.multiple_of(kbase * GRID_W, 2 * GRID_W))
        kinds.append(jnp.where(qi == 0, 0, jnp.where(qi == nq - 1, 2, 1)))
    steps = [(u, h) for u in range(NA_BLOCKS_PER_STEP) for h in range(NA_HEADS)]

    def logits(t):
        u, h = steps[t]
        lanes = slice(h * HEAD_DIM, (h + 1) * HEAD_DIM)
        qh = q_ref[u * NA_TQ:(u + 1) * NA_TQ, lanes]
        s_win = _dot_nt(k_ref[pl.ds(r0s[u], NA_TK), lanes], qh) + bias_ref[kinds[u], h]
        s_ctx = _dot_nt(ck_ref[:, lanes], qh)
        s_scr[t % 2, 0:NA_TK, :] = s_win
        s_scr[t % 2, NA_TK:NA_TK + cache_len, :] = s_ctx
        return jnp.maximum(jnp.max(s_win, axis=0, keepdims=True), jnp.max(s_ctx, axis=0, keepdims=True))

    m = logits(0)
    heads_t = []
    for t, (u, h) in enumerate(steps):
        nxt = logits(t + 1) if t + 1 < len(steps) else None
        vrows = slice(h * VT_ROWS, (h + 1) * VT_ROWS)
        p_win = jnp.exp2((s_scr[t % 2, 0:NA_TK, :] - m).astype(BF16))
        p_ctx = jnp.exp2((s_scr[t % 2, NA_TK:NA_TK + cache_len, :] - m).astype(BF16))
        acc = _dot(vt_ref[vrows, pl.ds(r0s[u], NA_TK)], p_win) + _dot(cvt_ref[vrows, :], p_ctx)
        heads_t.append(acc[0:HEAD_DIM] / acc[HEAD_DIM:HEAD_DIM + 1])
        if h == NA_HEADS - 1:
            o_ref[u * NA_TQ:(u + 1) * NA_TQ, :] = jnp.concatenate(heads_t, axis=0).T
            heads_t = []
        m = nxt


def _na_attention(q, k, vt, cache_k, cache_vt, bias, *, batch, seq):
    n = batch * seq
    rows = seq // GRID_W
    nq = rows // NA_QROWS
    cache_len = cache_k.shape[1]
    vt_rows = NA_HEADS * VT_ROWS

    nstep = nq // NA_BLOCKS_PER_STEP
    tq = NA_BLOCKS_PER_STEP * NA_TQ
    return pl.pallas_call(
        functools.partial(_na_kernel, rows=rows, cache_len=cache_len),
        grid=(batch, nstep),
        in_specs=[
            pl.BlockSpec((tq, BRANCH_W), lambda b, i: (b * nstep + i, 0)),
            pl.BlockSpec((None, seq, BRANCH_W), lambda b, i: (b, 0, 0)),
            pl.BlockSpec((None, vt_rows, seq), lambda b, i: (b, 0, 0)),
            pl.BlockSpec((None, cache_len, BRANCH_W), lambda b, i: (b, 0, 0)),
            pl.BlockSpec((None, vt_rows, cache_len), lambda b, i: (b, 0, 0)),
            _full((3, NA_HEADS, NA_TK, NA_TQ)),
        ],
        out_specs=pl.BlockSpec((tq, BRANCH_W), lambda b, i: (b * nstep + i, 0)),
        out_shape=jax.ShapeDtypeStruct((n, BRANCH_W), F32),
        scratch_shapes=[pltpu.VMEM((2, NA_TK + cache_len, NA_TQ), F32)],
        compiler_params=_params("parallel", "arbitrary"),
        name="na_latent",
    )(q, k.reshape(batch, seq, BRANCH_W), vt, cache_k, cache_vt, bias)


def _na_bias_tables(rel_bias, rows):
    nq = rows // NA_QROWS
    kh = min(NA_KH, rows)
    n_dr, n_dc = 2 * NA_KH - 1, 2 * NA_KW - 1
    cq = np.arange(GRID_W)
    start_c = np.clip(cq - NA_KW // 2, 0, GRID_W - NA_KW)
    col_ok = (cq[None, :] >= start_c[:, None]) & (cq[None, :] < start_c[:, None] + NA_KW)
    dc = np.clip(cq[None, :] - cq[:, None], -(NA_KW - 1), NA_KW - 1) + (NA_KW - 1)
    onehot = (dc.T.reshape(1, -1) == np.arange(n_dc)[:, None]).astype(np.float32)
    per_dr = jnp.einsum("hrc,cx->hrx", rel_bias.astype(F32), jnp.asarray(onehot), precision=lax.Precision.HIGHEST)
    per_dr = jnp.where(col_ok.T.reshape(1, 1, -1), per_dr * math.log2(math.e), NEG_INF)
    masked = jnp.full((NA_HEADS, 1, GRID_W * GRID_W), NEG_INF, F32)
    per_dr = jnp.concatenate([per_dr, masked], axis=1).reshape(NA_HEADS, n_dr + 1, GRID_W, GRID_W)
    tabs = []
    for blk in (0, 1, nq - 1):
        kbase = int(np.clip(blk * NA_QROWS - NA_KH // 2, 0, rows - NA_KROWS))
        r = blk * NA_QROWS + np.arange(NA_QROWS)
        start_r = np.clip(r - kh // 2, 0, rows - kh)
        kr = kbase + np.arange(NA_KROWS)
        row_ok = (kr[None, :] >= start_r[:, None]) & (kr[None, :] < start_r[:, None] + kh)
        dr = np.where(row_ok, kr[None, :] - r[:, None] + (NA_KH - 1), n_dr)
        tabs.append(jnp.concatenate(
            [jnp.concatenate([per_dr[:, int(dr[a, b])] for a in range(NA_QROWS)], axis=-1)
             for b in range(NA_KROWS)], axis=-2))
    return jnp.stack(tabs, axis=0)


def _gelu_tanh(y):
    return 0.5 * y * (1.0 + jnp.tanh(math.sqrt(2.0 / math.pi) * (y + 0.044715 * (y * y * y))))


S5_PARTS = 1


def _aligned(offset, multiple):
    return offset if isinstance(offset, int) else pl.multiple_of(offset, multiple)


def _s5_kernel(x0_ref, xn_ref, bmat_ref, cmat_ref, abar_ref, s0_ref, *rest, steps, reverse, nchunk):
    if reverse:
        xprev_ref, yprev_ref, dvec_ref, wglu_ref, bglu_ref, y_ref, sfin_ref, bu0, bu1, sb0, sb1, st_ref = rest
    else:
        y_ref, sfin_ref, bu0, bu1, sb0, sb1, st_ref = rest
    j = pl.program_id(1)

    @pl.when(j == 0)
    def _():
        st_ref[...] = s0_ref[...]
        sb0[...] = jnp.zeros_like(sb0)
        bu0[...] = _dot(x0_ref[...].astype(BF16), bmat_ref[...])

    pair_rows = 2 * SUBLANES
    part_rows = steps * SUBLANES // S5_PARTS
    pairs = part_rows // pair_rows

    def stage(bu_cur, bu_nxt, sb_prev, sb_cur):
        a_re = jnp.broadcast_to(abar_ref[:, 0:SSM_N], (SUBLANES, SSM_N))
        a_im = jnp.broadcast_to(abar_ref[:, SSM_N:], (SUBLANES, SSM_N))
        old_re, old_im = st_ref[:, 0:SSM_N], st_ref[:, SSM_N:]

        def part(q, carry):
            s_re, s_im = carry
            mrows = pl.ds(_aligned(q * part_rows, part_rows), part_rows)
            y = _dot(sb_prev[mrows, :], cmat_ref[...])
            bu_nxt[mrows, :] = _dot(xn_ref[mrows, :].astype(BF16), bmat_ref[...])
            if reverse:
                y = yprev_ref[mrows, :] + y + dvec_ref[...] * xprev_ref[mrows, :]
                y = _gelu_tanh(y)
                y = y * jax.nn.sigmoid(_dot(y.astype(BF16), wglu_ref[...]) + bglu_ref[...])
            y_ref[mrows, :] = y

            base = ((S5_PARTS - 1 - q) if reverse else q) * part_rows
            for i in range(pairs):
                r0 = _aligned(base + ((pairs - 1 - i) if reverse else i) * pair_rows, pair_rows)
                out_re, out_im = [None, None], [None, None]
                for half in ((1, 0) if reverse else (0, 1)):
                    rr = _aligned(r0 + half * SUBLANES, SUBLANES)
                    n_re = a_re * s_re - a_im * s_im + bu_cur[pl.ds(rr, SUBLANES), 0:SSM_N]
                    n_im = a_re * s_im + a_im * s_re + bu_cur[pl.ds(rr, SUBLANES), SSM_N:]
                    out_re[half], out_im[half] = n_re, n_im
                    s_re, s_im = n_re, n_im
                sb_cur[pl.ds(r0, pair_rows), 0:SSM_N] = jnp.concatenate(out_re, axis=0).astype(BF16)
                sb_cur[pl.ds(r0, pair_rows), SSM_N:] = jnp.concatenate(out_im, axis=0).astype(BF16)
            return s_re, s_im

        if S5_PARTS == 1:
            s_re, s_im = part(0, (old_re, old_im))
        else:
            s_re, s_im = lax.fori_loop(0, S5_PARTS, part, (old_re, old_im))
        live = j < nchunk
        st_ref[:, 0:SSM_N] = jnp.where(live, s_re, old_re)
        st_ref[:, SSM_N:] = jnp.where(live, s_im, old_im)

    @pl.when(j % 2 == 0)
    def _():
        stage(bu0, bu1, sb0, sb1)

    @pl.when(j % 2 == 1)
    def _():
        stage(bu1, bu0, sb1, sb0)

    sfin_ref[...] = st_ref[...]


def _s5_pass(x3, sp, s0, yprev, lw, *, steps, reverse):
    halves, rows, _ = x3.shape
    nchunk = rows // (steps * SUBLANES)
    tr = steps * SUBLANES

    def chunk_of(step):
        step = jnp.clip(step, 0, nchunk - 1)
        return (nchunk - 1 - step) if reverse else step

    first = lambda h, j: (h, chunk_of(0), 0)
    nxt = lambda h, j: (h, chunk_of(j + 1), 0)
    prev = lambda h, j: (h, chunk_of(j - 1), 0)

    in_specs = [
        pl.BlockSpec((None, tr, BRANCH_W), first),
        pl.BlockSpec((None, tr, BRANCH_W), nxt),
        _full((BRANCH_W, 2 * SSM_N)),
        _full((2 * SSM_N, BRANCH_W)),
        _full((1, 2 * SSM_N)),
        pl.BlockSpec((None, SUBLANES, 2 * SSM_N), lambda h, j: (h, 0, 0)),
    ]
    args = [x3, x3, sp["bmat"], sp["cmat"], sp["abar"], s0]
    if reverse:
        in_specs += [pl.BlockSpec((None, tr, BRANCH_W), prev), pl.BlockSpec((None, tr, BRANCH_W), prev),
                     _full((1, BRANCH_W)), _full((BRANCH_W, BRANCH_W)), _full((1, BRANCH_W))]
        args += [x3, yprev, lw["ssm_d"], lw["w_glu"], lw["b_glu"]]
    return pl.pallas_call(
        functools.partial(_s5_kernel, steps=steps, reverse=reverse, nchunk=nchunk),
        grid=(halves, nchunk + 1),
        in_specs=in_specs,
        out_specs=[pl.BlockSpec((None, tr, BRANCH_W), prev),
                   pl.BlockSpec((None, SUBLANES, 2 * SSM_N), lambda h, j: (h, 0, 0))],
        out_shape=[jax.ShapeDtypeStruct(x3.shape, F32),
                   jax.ShapeDtypeStruct((halves, SUBLANES, 2 * SSM_N), F32)],
        scratch_shapes=[pltpu.VMEM((tr, 2 * SSM_N), F32), pltpu.VMEM((tr, 2 * SSM_N), F32),
                        pltpu.VMEM((tr, 2 * SSM_N), BF16), pltpu.VMEM((tr, 2 * SSM_N), BF16),
                        pltpu.VMEM((SUBLANES, 2 * SSM_N), F32)],
        compiler_params=_params("parallel", "arbitrary"),
        name="s5_reverse" if reverse else "s5_forward",
    )(*args)


def _s5_discretise(a_re, a_im, log_dt, b_re, b_im, c_re, c_im):
    dt = jnp.exp(log_dt)[:, None]
    mag = jnp.exp(dt * a_re)
    ab_re, ab_im = mag * jnp.cos(dt * a_im), mag * jnp.sin(dt * a_im)
    den = a_re * a_re + a_im * a_im
    nr, ni = ab_re - 1.0, ab_im
    coef_re = ((nr * a_re + ni * a_im) / den)[..., None]
    coef_im = ((ni * a_re - nr * a_im) / den)[..., None]
    bb_re = coef_re * b_re - coef_im * b_im
    bb_im = coef_re * b_im + coef_im * b_re
    eye = jnp.eye(SSM_G, dtype=F32)

    def in_blocks(bb):
        return jnp.einsum("gpi,gk->gikp", bb, eye).reshape(SSM_G * SSM_H, SSM_N)

    def out_blocks(cc):
        return jnp.einsum("gip,gk->gpki", cc, eye).reshape(SSM_N, SSM_G * SSM_H)

    bmat = jnp.concatenate([in_blocks(bb_re), in_blocks(bb_im)], axis=1).astype(BF16)
    cmat = jnp.concatenate([out_blocks(c_re), -out_blocks(c_im)], axis=0).astype(BF16)
    abar = jnp.concatenate([ab_re.reshape(1, SSM_N), ab_im.reshape(1, SSM_N)], axis=1)
    return dict(bmat=bmat, cmat=cmat, abar=abar)


def _state_to_rows(s):
    b = s.shape[0]
    flat = jnp.concatenate([s[..., 0].reshape(b, SSM_N), s[..., 1].reshape(b, SSM_N)], axis=1)
    return flat.reshape(b // SUBLANES, SUBLANES, 2 * SSM_N)


def _rows_to_state(r):
    b = r.shape[0] * SUBLANES
    flat = r.reshape(b, 2, SSM_G, SSM_P)
    return jnp.stack([flat[:, 0], flat[:, 1]], axis=-1)


MERGE_SUB_ROWS = 256


def _merge_kernel(x_ref, g_ref, scale_ref, shift_ref, gate_ref, ya_ref, yb_ref, yc_ref, yd_ref,
                  wz_ref, wg_ref, wbr_ref, wout_ref, fg_ref, o_ref, *, last):
    tm = x_ref.shape[0]
    sub = min(tm, MERGE_SUB_ROWS)
    for r0 in range(0, tm, sub):
        rows = slice(r0, r0 + sub)
        x = x_ref[rows, :]
        hb = _modulated_norm(x, g_ref[...], scale_ref[...], shift_ref[...]).astype(BF16)
        z = _dot(hb, wz_ref[...])
        merged = None
        for n, y_ref in enumerate((ya_ref, yb_ref, yc_ref, yd_ref)):
            zn = z[:, n * BRANCH_W:(n + 1) * BRANCH_W]
            yn = y_ref[rows, :] * (zn * jax.nn.sigmoid(zn))
            t = _dot(yn.astype(BF16), wbr_ref[n * BRANCH_W:(n + 1) * BRANCH_W, :])
            gn = _dot(hb, wg_ref[:, n * D_MODEL:(n + 1) * D_MODEL])
            term = jax.nn.sigmoid(gn) * t
            merged = term if merged is None else merged + term
        out = x + gate_ref[...] * _dot(merged.astype(BF16), wout_ref[...])
        if last:
            out = (out * lax.rsqrt(jnp.mean(out * out, axis=-1, keepdims=True) + EPS)) * fg_ref[...]
        o_ref[rows, :] = out


def _merge(x, mod_scale, mod_shift, mod_gate, ya, yb, yc3, yd, lw, final_g, *, batch, seq, tm, last, name):
    n = batch * seq
    tiles_per_seq = seq // tm
    per_batch_mod = mod_scale.shape[0] > 1

    def mod_idx(i):
        return ((i // tiles_per_seq) if per_batch_mod else 0, 0, 0)

    def yc_idx(i):
        b = i // tiles_per_seq
        return (b // SUBLANES, i % tiles_per_seq, b % SUBLANES)

    row = lambda i: (i, 0)
    ytile = pl.BlockSpec((tm, BRANCH_W), row)
    return pl.pallas_call(
        functools.partial(_merge_kernel, last=last),
        grid=(n // tm,),
        in_specs=[
            pl.BlockSpec((tm, D_MODEL), row),
            _full((1, D_MODEL)),
            pl.BlockSpec((None, 1, D_MODEL), mod_idx),
            pl.BlockSpec((None, 1, D_MODEL), mod_idx),
            pl.BlockSpec((None, 1, D_MODEL), mod_idx),
            ytile, ytile,
            pl.BlockSpec((None, tm, BRANCH_W), yc_idx),
            ytile,
            _full((D_MODEL, 4 * BRANCH_W)),
            _full((D_MODEL, 4 * D_MODEL)),
            _full((4 * BRANCH_W, D_MODEL)),
            _full((D_MODEL, D_MODEL)),
            _full((1, D_MODEL)),
        ],
        out_specs=pl.BlockSpec((tm, D_MODEL), row),
        out_shape=jax.ShapeDtypeStruct((n, D_MODEL), F32),
        compiler_params=_params("parallel"),
        name=name,
    )(x, lw["norm_g"], mod_scale, mod_shift, mod_gate, ya, yb, yc3, yd,
      lw["w_z"], lw["w_g"], lw["w_branch"], lw["w_out"], final_g)


def _rope_tables(seq):
    t = np.arange(seq)
    row = (t // GRID_W).astype(np.float32)
    col = (t % GRID_W).astype(np.float32)
    nf = HEAD_DIM // 4
    freqs = jnp.asarray(ROPE_BASE, F32) ** (-jnp.arange(nf, dtype=F32) / nf)
    ang_r = jnp.asarray(row)[:, None] * freqs[None, :]
    ang_c = jnp.asarray(col)[:, None] * freqs[None, :]
    cos = jnp.concatenate([jnp.cos(ang_r)] * 2 + [jnp.cos(ang_c)] * 2, axis=1)
    sin = jnp.concatenate([-jnp.sin(ang_r), jnp.sin(ang_r), -jnp.sin(ang_c), jnp.sin(ang_c)], axis=1)
    return jnp.tile(cos, (1, GQA_HEADS)), jnp.tile(sin, (1, GQA_HEADS))


def _layer_weights(l, norm_g, w_in, gm_v_g, gm_ws, gm_b, ssm_d, w_glu, b_glu, gqa_q_g, gqa_k_g, w_branch, w_out):
    w = w_in[l]
    w_proj = jnp.concatenate([w[:, 0:512], w[:, 768:1536], w[:, 1792:2048], w[:, 2304:2816]], axis=1).astype(BF16)
    w_z = jnp.concatenate([w[:, 512:768], w[:, 1536:1792], w[:, 2048:2304], w[:, 2816:3072]], axis=1).astype(BF16)
    return dict(
        norm_g=norm_g[l].reshape(1, D_MODEL),
        w_proj=w_proj,
        w_z=w_z,
        w_g=w[:, 3072:].astype(BF16),
        gm_v_g=gm_v_g[l].reshape(1, BRANCH_W),
        gm_w=gm_ws[l].reshape(GM_GROUPS * CHUNK, CHUNK).astype(BF16),
        gm_bias=jnp.repeat(gm_b[l].T, HEAD_DIM, axis=1),
        ssm_d=ssm_d[l].reshape(1, BRANCH_W),
        w_glu=w_glu[l].astype(BF16),
        b_glu=b_glu[l].reshape(1, BRANCH_W),
        gqa_q_g=jnp.tile(gqa_q_g[l], GQA_HEADS).reshape(1, BRANCH_W),
        gqa_k_g=jnp.tile(gqa_k_g[l], GQA_KV).reshape(1, 128),
        w_branch=w_branch[l].reshape(4 * BRANCH_W, D_MODEL).astype(BF16),
        w_out=w_out[l].astype(BF16),
    )


def _trunk_layer(x, mods, lw, sps, consts, final_g, cache, *, batch, seq, latent, last):
    scale, shift, gate = mods
    tm = 1024 if latent else 256
    ya, bq, bk, bv, cx, dq, dk, dv = _inproj(x, scale, shift, lw, consts, batch=batch, seq=seq, tm=tm, latent=latent)
    halves = batch // SUBLANES
    cx3 = cx.reshape(halves, seq * SUBLANES, BRANCH_W)
    if latent:
        yb = _na_attention(bq, bk, bv, cache["na_k"], cache["na_vt"], cache["na_bias"], batch=batch, seq=seq)
        yd = _block_attention(dq, dk, dv, cache["gqa_k"], cache["gqa_vt"], batch=batch, seq=seq,
                              kvh=GQA_KV, rep=GQA_HEADS // GQA_KV, tq=256, ck=512, name="gqa_latent")
        s0 = cache["ssm"]
    else:
        yb = _block_attention(bq, bk, _vt_layout(bv, batch, seq, NA_HEADS), None, None, batch=batch, seq=seq,
                              kvh=NA_HEADS, rep=1, tq=seq, ck=seq, name="na_ctx")
        yd = _block_attention(dq, dk, _vt_layout(dv, batch, seq, GQA_KV), None, None, batch=batch, seq=seq,
                              kvh=GQA_KV, rep=GQA_HEADS // GQA_KV, tq=seq, ck=seq, name="gqa_ctx")
        zero = jnp.zeros((halves, SUBLANES, 2 * SSM_N), F32)
        s0 = (zero, zero)
    steps = 128 if latent else 64
    yf, sf = _s5_pass(cx3, sps[0], s0[0], None, lw, steps=steps, reverse=False)
    yc3, sr = _s5_pass(cx3, sps[1], s0[1], yf, lw, steps=steps, reverse=True)
    yc3 = yc3.reshape(halves, seq, SUBLANES * BRANCH_W)
    x_new = _merge(x, scale, shift, gate, ya, yb, yc3, yd, lw, final_g, batch=batch, seq=seq, tm=tm, last=last,
                   name="merge_latent" if latent else "merge_ctx")
    return x_new, (bk, bv, dk, dv, sf, sr)


def kernel(x_prompt, x_sample, c, cache_na_k, cache_na_v, cache_gqa_k, cache_gqa_v, state_ssm, c_ctx,
           norm_g, w_ada, b_ada, w_in, gm_v_g, gm_ws, gm_b, na_rel_bias, ssm_a_re, ssm_a_im, ssm_log_dt,
           ssm_b_re, ssm_b_im, ssm_c_re, ssm_c_im, ssm_d, w_glu, b_glu, gqa_q_g, gqa_k_g, w_branch, w_out,
           final_g):
    bc, lc, _ = x_prompt.shape
    bl, ll, _ = x_sample.shape
    past = cache_na_k.shape[2]

    n_rows = 16
    cond = jnp.zeros((n_rows, D_MODEL), F32).at[0].set(c_ctx).at[1:1 + bl].set(c)
    mod = _modulation(cond, w_ada, b_ada)

    def mods_of(l, lo, hi):
        m = mod[l, lo:hi].reshape(hi - lo, 1, 3 * D_MODEL)
        return m[..., 0:D_MODEL], m[..., D_MODEL:2 * D_MODEL], m[..., 2 * D_MODEL:]

    cos, sin = _rope_tables(ll)
    ones_blk = jnp.asarray(np.kron(np.eye(BRANCH_W // HEAD_DIM), np.ones((HEAD_DIM, HEAD_DIM))), BF16)
    consts = dict(rope_cos=cos, rope_sin=sin, ones_blk=ones_blk)
    fg = final_g.reshape(1, D_MODEL)

    xp = x_prompt.reshape(bc * lc, D_MODEL)
    xs = x_sample.reshape(bl * ll, D_MODEL)
    na_k_l, na_v_l, gqa_k_l, gqa_v_l, ssm_l = [], [], [], [], []
    for l in range(DEPTH):
        lw = _layer_weights(l, norm_g, w_in, gm_v_g, gm_ws, gm_b, ssm_d, w_glu, b_glu, gqa_q_g, gqa_k_g,
                            w_branch, w_out)
        sps = [_s5_discretise(ssm_a_re[l, d], ssm_a_im[l, d], ssm_log_dt[l, d], ssm_b_re[l, d], ssm_b_im[l, d],
                              ssm_c_re[l, d], ssm_c_im[l, d]) for d in range(2)]
        last = l == DEPTH - 1
        shift, scale, gate = mods_of(l, 0, 1)
        xp, (k_na, v_na, k_g, v_g, sf, sr) = _trunk_layer(
            xp, (scale, shift, gate), lw, sps, consts, fg, None, batch=bc, seq=lc, latent=False, last=last)
        na_k_l.append(k_na.reshape(bc, lc, NA_HEADS, HEAD_DIM))
        na_v_l.append(v_na.reshape(bc, lc, NA_HEADS, HEAD_DIM))
        gqa_k_l.append(k_g.reshape(bc, lc, GQA_KV, HEAD_DIM))
        gqa_v_l.append(v_g.reshape(bc, lc, GQA_KV, HEAD_DIM))
        ssm_l.append(jnp.stack([_rows_to_state(sf), _rows_to_state(sr)], axis=1))
        shift, scale, gate = mods_of(l, 1, 1 + bl)
        cache = dict(
            na_k=cache_na_k[:, l].reshape(bl, past, BRANCH_W).astype(BF16),
            na_vt=_vt_layout(cache_na_v[:, l], bl, past, NA_HEADS),
            gqa_k=cache_gqa_k[:, l].reshape(bl, past, GQA_KV * HEAD_DIM).astype(BF16),
            gqa_vt=_vt_layout(cache_gqa_v[:, l], bl, past, GQA_KV),
            ssm=(_state_to_rows(state_ssm[:, l, 0]), _state_to_rows(state_ssm[:, l, 1])),
            na_bias=_na_bias_tables(na_rel_bias[l], ll // GRID_W),
        )
        xs, _ = _trunk_layer(xs, (scale, shift, gate), lw, sps, consts, fg, cache,
                             batch=bl, seq=ll, latent=True, last=last)
    return (xp.reshape(bc, lc, D_MODEL), xs.reshape(bl, ll, D_MODEL),
            jnp.stack(na_k_l, axis=1), jnp.stack(na_v_l, axis=1),
            jnp.stack(gqa_k_l, axis=1), jnp.stack(gqa_v_l, axis=1), jnp.stack(ssm_l, axis=1))
```
